```python
import jax
import jax.numpy as jnp
from jax import lax
import numpy as np

D_MODEL = 1024
BATCH = 32
SEQ = 2048
DEPTH = 4

CTX_LEN = 256
GRID_W = 64
EPS = 1e-6
ROPE_THETA = 10000.0
Q_BLOCK = 128
N_ATTN_LAYERS = (DEPTH + 1) // 2
N_FC_LAYERS = DEPTH // 2

MLA_HEADS = 4
MLA_Q_RANK = D_MODEL // 2
MLA_KV_RANK = D_MODEL // 4
MLA_NOPE = 128
MLA_ROPE = 64
MLA_V = 128
MLA_SCALE = (MLA_NOPE + MLA_ROPE) ** -0.5

GQA_HEADS = 4
GQA_KV_HEADS = 2
GQA_GROUP = GQA_HEADS // GQA_KV_HEADS
GQA_HEAD_DIM = 128
GQA_SCALE = GQA_HEAD_DIM ** -0.5

FNET_GROUPS = 4
FNET_GROUP_W = 128
FNET_W = FNET_GROUPS * FNET_GROUP_W
CONV_CH = 512
CONV_K = 31

D_FF = 4 * D_MODEL
N_MOD = 6

ATTN_Q_W = MLA_Q_RANK + GQA_HEADS * GQA_HEAD_DIM
ATTN_KV_W = MLA_KV_RANK + MLA_ROPE + 2 * GQA_KV_HEADS * GQA_HEAD_DIM
ATTN_IN_W = ATTN_Q_W + ATTN_KV_W
ATTN_MIX_W = MLA_HEADS * MLA_V + GQA_HEADS * GQA_HEAD_DIM
FC_IN_W = FNET_W + 2 * CONV_CH
FC_MIX_W = FNET_W + CONV_CH

kernel_name = 'hybrid_mla_gqa_fnet_conformer_dit_block'


def rms_norm(x, g):
    xf = x.astype(jnp.float32)
    y = xf * lax.rsqrt(jnp.mean(xf * xf, axis=-1, keepdims=True) + EPS)
    return (y * g.astype(jnp.float32)).astype(x.dtype)


def layer_norm(x, g, b):
    xf = x.astype(jnp.float32)
    mu = jnp.mean(xf, axis=-1, keepdims=True)
    var = jnp.mean(jnp.square(xf - mu), axis=-1, keepdims=True)
    y = (xf - mu) * lax.rsqrt(var + EPS)
    return (y * g.astype(jnp.float32) + b.astype(jnp.float32)).astype(x.dtype)


def adaln_params(cond, w, b):
    m = jax.nn.silu(cond) @ w + b
    return jnp.split(m[:, None, :], N_MOD, axis=-1)


def modulate(h, shift, scale):
    return h * (1 + scale) + shift


def rope_1d(x, pos):
    half = x.shape[-1] // 2
    freqs = ROPE_THETA ** (-jnp.arange(half, dtype=jnp.float32) / half)
    ang = pos.astype(jnp.float32)[:, None] * freqs[None, :]
    cos = jnp.cos(ang)[None, :, None, :]
    sin = jnp.sin(ang)[None, :, None, :]
    xf = x.astype(jnp.float32)
    x1, x2 = xf[..., :half], xf[..., half:]
    return jnp.concatenate([x1 * cos - x2 * sin, x1 * sin + x2 * cos], axis=-1).astype(x.dtype)


def rope_2d(x, pos):
    if pos is None:
        return x
    row, col = pos
    half = x.shape[-1] // 2
    return jnp.concatenate([rope_1d(x[..., :half], row), rope_1d(x[..., half:], col)], axis=-1)


def attention(q, k, v, scale):
    b, s, hkv, g, dk = q.shape
    nb = s // Q_BLOCK
    qb = jnp.moveaxis(q.reshape(b, nb, Q_BLOCK, hkv, g, dk), 1, 0)

    def block(qblk):
        sc = jnp.einsum('bqhgd,bthd->bhgqt', qblk, k).astype(jnp.float32) * scale
        p = jax.nn.softmax(sc, axis=-1).astype(v.dtype)
        return jnp.einsum('bhgqt,bthe->bqhge', p, v)

    o = lax.map(block, qb)
    return jnp.moveaxis(o, 0, 1).reshape(b, s, hkv * g * v.shape[-1])


def attn_queries(pq, q_norm_g, w_uq, qk_norm_g, pos):
    b, t, _ = pq.shape
    cq = rms_norm(pq[..., :MLA_Q_RANK], q_norm_g)
    q = (cq @ w_uq).reshape(b, t, MLA_HEADS, MLA_NOPE + MLA_ROPE)
    q_mla = jnp.concatenate([q[..., :MLA_NOPE], rope_2d(q[..., MLA_NOPE:], pos)], axis=-1)
    q_gqa = rope_2d(rms_norm(pq[..., MLA_Q_RANK:].reshape(b, t, GQA_HEADS, GQA_HEAD_DIM), qk_norm_g), pos)
    return (q_mla.reshape(b, t, MLA_HEADS, 1, MLA_NOPE + MLA_ROPE),
            q_gqa.reshape(b, t, GQA_KV_HEADS, GQA_GROUP, GQA_HEAD_DIM))


def attn_keys_values(pkv, kv_norm_g, w_ukv, k_norm_g, pos):
    b, t, _ = pkv.shape
    o1 = MLA_KV_RANK
    o2 = o1 + MLA_ROPE
    o3 = o2 + GQA_KV_HEADS * GQA_HEAD_DIM
    ckv = rms_norm(pkv[..., :o1], kv_norm_g)
    kv = (ckv @ w_ukv).reshape(b, t, MLA_HEADS, MLA_NOPE + MLA_V)
    k_rope = rope_2d(pkv[..., o1:o2].reshape(b, t, 1, MLA_ROPE), pos)
    k_mla = jnp.concatenate([kv[..., :MLA_NOPE], jnp.broadcast_to(k_rope, (b, t, MLA_HEADS, MLA_ROPE))], axis=-1)
    v_mla = kv[..., MLA_NOPE:]
    k_gqa = rope_2d(rms_norm(pkv[..., o2:o3].reshape(b, t, GQA_KV_HEADS, GQA_HEAD_DIM), k_norm_g), pos)
    v_gqa = pkv[..., o3:].reshape(b, t, GQA_KV_HEADS, GQA_HEAD_DIM)
    return (k_mla, v_mla, k_gqa, v_gqa)


def attn_mix(q, kv, w_out):
    q_mla, q_gqa = q
    k_mla, v_mla, k_gqa, v_gqa = kv
    o_mla = attention(q_mla, k_mla, v_mla, MLA_SCALE)
    o_gqa = attention(q_gqa, k_gqa, v_gqa, GQA_SCALE)
    return jnp.concatenate([o_mla, o_gqa], axis=-1) @ w_out


def fourier_conv_mix(h, w_in, conv_w, conv_b, ln_g, ln_b, w_out):
    b, t, _ = h.shape
    p = h @ w_in
    f = p[..., :FNET_W].reshape(b, t, FNET_GROUPS, FNET_GROUP_W).astype(jnp.float32)
    f = jnp.fft.fft2(f, axes=(1, 3), norm='ortho').real.astype(h.dtype).reshape(b, t, FNET_W)
    a, gate = jnp.split(p[..., FNET_W:], 2, axis=-1)
    u = a * jax.nn.sigmoid(gate)
    u = lax.conv_general_dilated(u, conv_w[:, None, :].astype(u.dtype), (1,), [(CONV_K // 2, CONV_K // 2)],
                                 dimension_numbers=('NWC', 'WIO', 'NWC'), feature_group_count=CONV_CH) + conv_b
    u = jax.nn.silu(layer_norm(u, ln_g, ln_b))
    return jnp.concatenate([f, u], axis=-1) @ w_out


def sq_relu_mlp(h, w1, w2):
    return jnp.square(jax.nn.relu(h @ w1)) @ w2


def setup_inputs(seed: int = 0) -> dict:
    key = jax.random.key(seed)
    ks = jax.random.split(key, 25)
    f32 = jnp.float32
    d = D_MODEL
    na, nf = N_ATTN_LAYERS, N_FC_LAYERS

    def nrm(k, shape, scale):
        return jax.random.normal(k, shape, f32) * scale

    def gain(k, shape):
        return 1.0 + 0.05 * jax.random.normal(k, shape, f32)

    return {
        'x': nrm(ks[0], (BATCH, SEQ, d), 1.0),
        'c': nrm(ks[1], (BATCH, d), 1.0),
        'ctx': nrm(ks[2], (BATCH, CTX_LEN, d), 1.0),
        'c_ctx': nrm(ks[3], (d,), 1.0),
        'mod_w': nrm(ks[4], (DEPTH, d, N_MOD * d), 0.5 * d ** -0.5),
        'mod_b': nrm(ks[5], (DEPTH, N_MOD * d), 0.02),
        'norm1_g': gain(ks[6], (DEPTH, d)),
        'norm2_g': gain(ks[7], (DEPTH, d)),
        'attn_w_in': nrm(ks[8], (na, d, ATTN_IN_W), d ** -0.5),
        'mla_q_norm_g': gain(ks[9], (na, MLA_Q_RANK)),
        'mla_w_uq': nrm(ks[10], (na, MLA_Q_RANK, MLA_HEADS * (MLA_NOPE + MLA_ROPE)), MLA_Q_RANK ** -0.5),
        'mla_kv_norm_g': gain(ks[11], (na, MLA_KV_RANK)),
        'mla_w_ukv': nrm(ks[12], (na, MLA_KV_RANK, MLA_HEADS * (MLA_NOPE + MLA_V)), MLA_KV_RANK ** -0.5),
        'gqa_q_norm_g': gain(ks[13], (na, GQA_HEAD_DIM)),
        'gqa_k_norm_g': gain(ks[14], (na, GQA_HEAD_DIM)),
        'attn_w_out': nrm(ks[15], (na, ATTN_MIX_W, d), ATTN_MIX_W ** -0.5),
        'fc_w_in': nrm(ks[16], (nf, d, FC_IN_W), d ** -0.5),
        'conv_w': nrm(ks[17], (nf, CONV_K, CONV_CH), CONV_K ** -0.5),
        'conv_b': nrm(ks[18], (nf, CONV_CH), 0.02),
        'conv_ln_g': gain(ks[19], (nf, CONV_CH)),
        'conv_ln_b': nrm(ks[20], (nf, CONV_CH), 0.02),
        'fc_w_out': nrm(ks[21], (nf, FC_MIX_W, d), FC_MIX_W ** -0.5),
        'mlp_w1': nrm(ks[22], (DEPTH, d, D_FF), d ** -0.5),
        'mlp_w2': nrm(ks[23], (DEPTH, D_FF, d), D_FF ** -0.5),
        'final_g': gain(ks[24], (d,)),
    }


def reference(x, c, ctx, c_ctx, mod_w, mod_b, norm1_g, norm2_g, attn_w_in, mla_q_norm_g, mla_w_uq,
              mla_kv_norm_g, mla_w_ukv, gqa_q_norm_g, gqa_k_norm_g, attn_w_out, fc_w_in, conv_w, conv_b,
              conv_ln_g, conv_ln_b, fc_w_out, mlp_w1, mlp_w2, final_g):
    s = x.shape[1]
    rows = s // GRID_W
    row = jnp.repeat(jnp.arange(rows, dtype=jnp.int32), GRID_W)
    col = jnp.tile(jnp.arange(GRID_W, dtype=jnp.int32), rows)
    pos = (row, col)
    last_attn = ((DEPTH - 1) // 2) * 2

    for i in range(DEPTH):
        ctx_mode = 'full' if i < last_attn else ('kv' if i == last_attn else 'none')
        j = i // 2
        sh1, sc1, g1, sh2, sc2, g2 = adaln_params(c, mod_w[i], mod_b[i])
        h = modulate(rms_norm(x, norm1_g[i]), sh1, sc1)
        if ctx_mode != 'none':
            csh1, csc1, cg1, csh2, csc2, cg2 = adaln_params(c_ctx[None, :], mod_w[i], mod_b[i])
            hc = modulate(rms_norm(ctx, norm1_g[i]), csh1, csc1)

        if i % 2 == 0:
            q_p = (mla_q_norm_g[j], mla_w_uq[j], gqa_q_norm_g[j])
            kv_p = (mla_kv_norm_g[j], mla_w_ukv[j], gqa_k_norm_g[j])
            w_in = attn_w_in[j]
            if ctx_mode == 'full':
                pc = hc @ w_in
                pqc, pkvc = pc[..., :ATTN_Q_W], pc[..., ATTN_Q_W:]
            else:
                pkvc = hc @ w_in[:, ATTN_Q_W:]
            kvc = attn_keys_values(pkvc, *kv_p, None)
            px = h @ w_in
            qx = attn_queries(px[..., :ATTN_Q_W], *q_p, pos)
            kvx = attn_keys_values(px[..., ATTN_Q_W:], *kv_p, pos)
            kv_all = tuple(jnp.concatenate([a, bb], axis=1) for a, bb in zip(kvc, kvx))
            x = x + g1 * attn_mix(qx, kv_all, attn_w_out[j])
            if ctx_mode == 'full':
                qc = attn_queries(pqc, *q_p, None)
                ctx = ctx + cg1 * attn_mix(qc, kvc, attn_w_out[j])
        else:
            fc_p = (fc_w_in[j], conv_w[j], conv_b[j], conv_ln_g[j], conv_ln_b[j], fc_w_out[j])
            x = x + g1 * fourier_conv_mix(h, *fc_p)
            if ctx_mode == 'full':
                ctx = ctx + cg1 * fourier_conv_mix(hc, *fc_p)

        x = x + g2 * sq_relu_mlp(modulate(rms_norm(x, norm2_g[i]), sh2, sc2), mlp_w1[i], mlp_w2[i])
        if ctx_mode == 'full':
            ctx = ctx + cg2 * sq_relu_mlp(modulate(rms_norm(ctx, norm2_g[i]), csh2, csc2), mlp_w1[i], mlp_w2[i])

    return rms_norm(x, final_g)
```

```python
import functools

import jax
import jax.numpy as jnp
import numpy as np
from jax import lax
from jax.experimental import pallas as pl
from jax.experimental.pallas import tpu as pltpu

F32 = jnp.float32
BF16 = jnp.bfloat16

D_MODEL = 1024
GRID_W = 64
EPS = 1e-6
ROPE_THETA = 10000.0
N_MOD = 6

MLA_HEADS = 4
MLA_Q_RANK = D_MODEL // 2
MLA_KV_RANK = D_MODEL // 4
MLA_NOPE = 128
MLA_ROPE = 64
MLA_V = 128
MLA_QK_PAD = 256
MLA_SCALE = (MLA_NOPE + MLA_ROPE) ** -0.5

GQA_HEADS = 4
GQA_KV_HEADS = 2
GQA_GROUP = GQA_HEADS // GQA_KV_HEADS
GQA_HEAD_DIM = 128
GQA_SCALE = GQA_HEAD_DIM ** -0.5

FNET_GROUPS = 4
FNET_GROUP_W = 128
FNET_W = FNET_GROUPS * FNET_GROUP_W
CONV_CH = 512
CONV_K = 31
CONV_HALO = 16
D_FF = 4 * D_MODEL

ATTN_Q_W = MLA_Q_RANK + GQA_HEADS * GQA_HEAD_DIM
ATTN_IN_PAD_W = 1920
COL_QG = MLA_Q_RANK
COL_CKV = COL_QG + GQA_HEADS * GQA_HEAD_DIM
COL_KG = COL_CKV + MLA_KV_RANK
COL_VG = COL_KG + GQA_KV_HEADS * GQA_HEAD_DIM
COL_KR = COL_VG + GQA_KV_HEADS * GQA_HEAD_DIM
KR_W = 128

V7X_VMEM_LIMIT = 56 * 1024 * 1024
TOKEN_TILE = 512
ATTN_Q_TILE = 512
DFT_ROW_TILE = 512
CONV_TILE = 256
CONV_SUB = 64
FF_CHUNK = 1024
ADALN_COL_TILE = 1536


def _params(n_axes):
    return pltpu.CompilerParams(dimension_semantics=("arbitrary",) * n_axes, vmem_limit_bytes=V7X_VMEM_LIMIT)


def _const_spec(shape):
    zeros = (0,) * len(shape)
    return pl.BlockSpec(shape, lambda *_: zeros, pipeline_mode=pl.Buffered(1))


def _rms(x, g):
    return x * lax.rsqrt(jnp.mean(x * x, axis=-1, keepdims=True) + EPS) * g


def _norm_mod(x, g, shift, scale):
    return _rms(x, g) * (1.0 + scale) + shift


def _swap_halves(x, blk):
    w = x.shape[-1]
    ax = x.ndim - 1
    lane = lax.broadcasted_iota(jnp.int32, x.shape, ax)
    first = (lane & (2 * blk - 1)) < blk
    return jnp.where(first, pltpu.roll(x, w - blk, ax), pltpu.roll(x, blk, ax))


def _rope(x, cos, sin, blk):
    return x * cos + _swap_halves(x, blk) * sin


def _adaln_kernel(c_ref, w_ref, b_ref, o_ref):
    c = c_ref[...]
    s = (c * jax.nn.sigmoid(c)).astype(BF16)
    o_ref[...] = jnp.dot(s, w_ref[...].astype(BF16), preferred_element_type=F32) + b_ref[...]


def _adaln(cond, mod_w, mod_b):
    depth, d, n = mod_w.shape
    rows = cond.shape[0]
    tn = ADALN_COL_TILE if n % ADALN_COL_TILE == 0 else n
    return pl.pallas_call(
        _adaln_kernel,
        grid=(depth, n // tn),
        in_specs=[
            pl.BlockSpec((rows, d), lambda l, j: (0, 0)),
            pl.BlockSpec((None, d, tn), lambda l, j: (l, 0, j)),
            pl.BlockSpec((None, 1, tn), lambda l, j: (l, 0, j)),
        ],
        out_specs=pl.BlockSpec((None, rows, tn), lambda l, j: (l, 0, j)),
        out_shape=jax.ShapeDtypeStruct((depth, rows, n), F32),
        compiler_params=_params(2),
        name="adaln",
    )(cond, mod_w, mod_b.reshape(depth, 1, n))


def _attn_in_kernel(x_ref, mod_ref, n1g_ref, win_ref, qng_ref, wuq_ref, kvng_ref, wukv_ref, gq_ref, gk_ref,
                    cm_ref, sm_ref, cg_ref, sg_ref,
                    qm_ref, qg_ref, ktm_ref, ktg_ref, vm_ref, vg_ref):
    h = _norm_mod(x_ref[...], n1g_ref[...], mod_ref[0:1, :], mod_ref[1:2, :]).astype(BF16)
    p = jnp.dot(h, win_ref[...], preferred_element_type=F32)
    cm, sm = cm_ref[...], sm_ref[...]
    cg, sg = cg_ref[...], sg_ref[...]

    cq = _rms(p[:, 0:MLA_Q_RANK], qng_ref[...]).astype(BF16)
    qm = jnp.dot(cq, wuq_ref[...], preferred_element_type=F32)
    for hd in range(MLA_HEADS):
        sl = slice(hd * MLA_QK_PAD, (hd + 1) * MLA_QK_PAD)
        qm_ref[:, sl] = (_rope(qm[:, sl], cm, sm, MLA_ROPE // 4) * MLA_SCALE).astype(BF16)

    for hd in range(GQA_HEADS):
        sl = slice(hd * GQA_HEAD_DIM, (hd + 1) * GQA_HEAD_DIM)
        y = _rms(p[:, COL_QG + hd * GQA_HEAD_DIM:COL_QG + (hd + 1) * GQA_HEAD_DIM], gq_ref[...])
        qg_ref[:, sl] = (_rope(y, cg, sg, GQA_HEAD_DIM // 4) * GQA_SCALE).astype(BF16)

    ckv = _rms(p[:, COL_CKV:COL_CKV + MLA_KV_RANK], kvng_ref[...])
    kr = _rope(p[:, COL_KR:COL_KR + KR_W], cm[:, MLA_NOPE:MLA_NOPE + KR_W], sm[:, MLA_NOPE:MLA_NOPE + KR_W],
               MLA_ROPE // 4)
    lhs = jnp.concatenate([ckv, kr], axis=-1).astype(BF16)
    kv = jnp.dot(lhs, wukv_ref[...], preferred_element_type=F32)
    kw = MLA_HEADS * MLA_QK_PAD
    ktm_ref[...] = kv[:, :kw].T.astype(BF16)
    vm_ref[...] = kv[:, kw:].astype(BF16)

    for hd in range(GQA_KV_HEADS):
        y = _rms(p[:, COL_KG + hd * GQA_HEAD_DIM:COL_KG + (hd + 1) * GQA_HEAD_DIM], gk_ref[...])
        ktg_ref[hd * GQA_HEAD_DIM:(hd + 1) * GQA_HEAD_DIM, :] = _rope(y, cg, sg, GQA_HEAD_DIM // 4).T.astype(BF16)
    vg_ref[...] = p[:, COL_VG:COL_VG + GQA_KV_HEADS * GQA_HEAD_DIM].astype(BF16)


def _attn_in(x, mod, per_batch_mod, n1g, w, tables):
    b, t, d = x.shape
    tm = min(TOKEN_TILE, t)
    mod_idx = (lambda i, j: (i, 0, 0)) if per_batch_mod else (lambda i, j: (0, 0, 0))
    tok = lambda width: pl.BlockSpec((None, tm, width), lambda i, j: (i, j, 0))
    tab = lambda width: pl.BlockSpec((tm, width), lambda i, j: (j, 0))
    qk_w = MLA_HEADS * MLA_QK_PAD
    out_shapes = (
        jax.ShapeDtypeStruct((b, t, qk_w), BF16),
        jax.ShapeDtypeStruct((b, t, GQA_HEADS * GQA_HEAD_DIM), BF16),
        jax.ShapeDtypeStruct((b, qk_w, t), BF16),
        jax.ShapeDtypeStruct((b, GQA_KV_HEADS * GQA_HEAD_DIM, t), BF16),
        jax.ShapeDtypeStruct((b, t, MLA_HEADS * MLA_V), BF16),
        jax.ShapeDtypeStruct((b, t, GQA_KV_HEADS * GQA_HEAD_DIM), BF16),
    )
    out_specs = (
        tok(qk_w), tok(GQA_HEADS * GQA_HEAD_DIM),
        pl.BlockSpec((None, qk_w, tm), lambda i, j: (i, 0, j)),
        pl.BlockSpec((None, GQA_KV_HEADS * GQA_HEAD_DIM, tm), lambda i, j: (i, 0, j)),
        tok(MLA_HEADS * MLA_V), tok(GQA_KV_HEADS * GQA_HEAD_DIM),
    )
    return pl.pallas_call(
        _attn_in_kernel,
        grid=(b, t // tm),
        in_specs=[
            tok(d),
            pl.BlockSpec((None, N_MOD, d), mod_idx),
            _const_spec((1, d)),
            _const_spec(w["w_in"].shape),
            _const_spec((1, MLA_Q_RANK)),
            _const_spec(w["w_uq"].shape),
            _const_spec((1, MLA_KV_RANK)),
            _const_spec(w["w_ukv"].shape),
            _const_spec((1, GQA_HEAD_DIM)),
            _const_spec((1, GQA_HEAD_DIM)),
            tab(MLA_QK_PAD), tab(MLA_QK_PAD), tab(GQA_HEAD_DIM), tab(GQA_HEAD_DIM),
        ],
        out_specs=out_specs,
        out_shape=out_shapes,
        compiler_params=_params(2),
        name="attn_in",
    )(x, mod, n1g, w["w_in"], w["q_norm_g"], w["w_uq"], w["kv_norm_g"], w["w_ukv"], w["gq"], w["gk"], *tables)


def _attn_kernel(*refs, nseg):
    q_ref, o_ref = refs[0], refs[-1]
    q = q_ref[...]
    scores = [jnp.dot(q, refs[1 + 2 * i][...], preferred_element_type=F32) for i in range(nseg)]
    m = functools.reduce(jnp.maximum, [jnp.max(s, axis=-1, keepdims=True) for s in scores])
    acc, denom = None, None
    for i in range(nseg):
        pr = jnp.exp(scores[i] - m)
        part_l = jnp.sum(pr, axis=-1, keepdims=True)
        part_o = jnp.dot(pr.astype(BF16), refs[2 + 2 * i][...], preferred_element_type=F32)
        denom = part_l if denom is None else denom + part_l
        acc = part_o if acc is None else acc + part_o
    o_ref[...] = (acc / denom).astype(BF16)


def _attention(q, segs, heads, kv_group, dk, dv):
    b, t, _ = q.shape
    tq = min(ATTN_Q_TILE, t)
    in_specs = [pl.BlockSpec((None, tq, dk), lambda i, h, j: (i, j, h))]
    args = [q]
    for kt, v in segs:
        tk = v.shape[1]
        in_specs.append(pl.BlockSpec((None, dk, tk), lambda i, h, j: (i, h // kv_group, 0)))
        in_specs.append(pl.BlockSpec((None, tk, dv), lambda i, h, j: (i, 0, h // kv_group)))
        args += [kt, v]
    return pl.pallas_call(
        functools.partial(_attn_kernel, nseg=len(segs)),
        grid=(b, heads, t // tq),
        in_specs=in_specs,
        out_specs=pl.BlockSpec((None, tq, dv), lambda i, h, j: (i, j, h)),
        out_shape=jax.ShapeDtypeStruct((b, t, heads * dv), BF16),
        compiler_params=_params(3),
        name="attention",
    )(*args)


def _post_kernel(x_ref, ma_ref, mb_ref, mod_ref, wo_ref, n2g_ref, w1_ref, w2_ref, fg_ref, o_ref, *, final):
    half = ma_ref.shape[-1]
    mix = (jnp.dot(ma_ref[...], wo_ref[0:half, :], preferred_element_type=F32)
           + jnp.dot(mb_ref[...], wo_ref[half:2 * half, :], preferred_element_type=F32))
    x1 = x_ref[...] + mod_ref[2:3, :] * mix
    hn = _norm_mod(x1, n2g_ref[...], mod_ref[3:4, :], mod_ref[4:5, :]).astype(BF16)
    acc = None
    for c in range(w1_ref.shape[1] // FF_CHUNK):
        sl = slice(c * FF_CHUNK, (c + 1) * FF_CHUNK)
        hc = jnp.maximum(jnp.dot(hn, w1_ref[:, sl], preferred_element_type=F32), 0.0)
        part = jnp.dot((hc * hc).astype(BF16), w2_ref[sl, :], preferred_element_type=F32)
        acc = part if acc is None else acc + part
    x2 = x1 + mod_ref[5:6, :] * acc
    if final:
        x2 = _rms(x2, fg_ref[...])
    o_ref[...] = x2


def _post(x, mix_a, mix_b, mod, per_batch_mod, w_out, n2g, w1, w2, final_g, final):
    b, t, d = x.shape
    tm = min(TOKEN_TILE, t)
    half = w_out.shape[0] // 2
    mod_idx = (lambda i, j: (i, 0, 0)) if per_batch_mod else (lambda i, j: (0, 0, 0))
    (arr_a, blk_a), (arr_b, blk_b) = mix_a, mix_b
    return pl.pallas_call(
        functools.partial(_post_kernel, final=final),
        grid=(b, t // tm),
        in_specs=[
            pl.BlockSpec((None, tm, d), lambda i, j: (i, j, 0)),
            pl.BlockSpec((None, tm, half), lambda i, j: (i, j, blk_a)),
            pl.BlockSpec((None, tm, half), lambda i, j: (i, j, blk_b)),
            pl.BlockSpec((None, N_MOD, d), mod_idx),
            _const_spec(w_out.shape),
            _const_spec((1, d)),
            _const_spec(w1.shape),
            _const_spec(w2.shape),
            _const_spec((1, d)),
        ],
        out_specs=pl.BlockSpec((None, tm, d), lambda i, j: (i, j, 0)),
        out_shape=jax.ShapeDtypeStruct((b, t, d), F32),
        compiler_params=_params(2),
        name="post_mlp",
    )(x, arr_a, arr_b, mod, w_out, n2g, w1, w2, final_g)


def _fc_in_kernel(x_ref, mod_ref, n1g_ref, win_ref, cs_ref, fcs_ref, u_ref):
    h = _norm_mod(x_ref[...], n1g_ref[...], mod_ref[0:1, :], mod_ref[1:2, :]).astype(BF16)
    p = jnp.dot(h, win_ref[...], preferred_element_type=F32)
    for g in range(FNET_GROUPS):
        sl = slice(g * FNET_GROUP_W, (g + 1) * FNET_GROUP_W)
        r = jnp.dot(p[:, sl].astype(BF16), cs_ref[...], preferred_element_type=F32)
        fcs_ref[0, :, sl] = r[:, :FNET_GROUP_W].astype(BF16)
        fcs_ref[1, :, sl] = r[:, FNET_GROUP_W:].astype(BF16)
    u_ref[...] = p[:, FNET_W:FNET_W + CONV_CH] * jax.nn.sigmoid(p[:, FNET_W + CONV_CH:])


def _fc_in(x, mod, per_batch_mod, n1g, w_in, cs):
    b, t, d = x.shape
    tm = min(TOKEN_TILE, t)
    mod_idx = (lambda i, j: (i, 0, 0)) if per_batch_mod else (lambda i, j: (0, 0, 0))
    return pl.pallas_call(
        _fc_in_kernel,
        grid=(b, t // tm),
        in_specs=[
            pl.BlockSpec((None, tm, d), lambda i, j: (i, j, 0)),
            pl.BlockSpec((None, N_MOD, d), mod_idx),
            _const_spec((1, d)),
            _const_spec(w_in.shape),
            _const_spec(cs.shape),
        ],
        out_specs=(
            pl.BlockSpec((None, 2, tm, FNET_W), lambda i, j: (i, 0, j, 0)),
            pl.BlockSpec((None, tm, CONV_CH), lambda i, j: (i, j, 0)),
        ),
        out_shape=(
            jax.ShapeDtypeStruct((b, 2, t, FNET_W), BF16),
            jax.ShapeDtypeStruct((b, t, CONV_CH), F32),
        ),
        compiler_params=_params(2),
        name="fc_in",
    )(x, mod, n1g, w_in, cs)


def _dft_kernel(w_ref, f_ref, o_ref):
    o_ref[...] = jnp.dot(w_ref[...], f_ref[...], preferred_element_type=F32).astype(BF16)


def _seq_dft(fcs, w_dft):
    b, _, t, w = fcs.shape
    tk = min(DFT_ROW_TILE, t)
    return pl.pallas_call(
        _dft_kernel,
        grid=(t // tk, b),
        in_specs=[
            pl.BlockSpec((tk, 2 * t), lambda k, i: (k, 0)),
            pl.BlockSpec((None, 2 * t, w), lambda k, i: (i, 0, 0)),
        ],
        out_specs=pl.BlockSpec((None, tk, w), lambda k, i: (i, k, 0)),
        out_shape=jax.ShapeDtypeStruct((b, t, w), BF16),
        compiler_params=_params(2),
        name="seq_dft",
    )(w_dft, fcs.reshape(b, 2 * t, w))


def _conv_kernel(up_ref, uc_ref, un_ref, w_ref, b_ref, lg_ref, lb_ref, o_ref, pad_ref):
    j, nj = pl.program_id(1), pl.num_programs(1)
    tc = uc_ref.shape[0]
    pad_ref[0:CONV_HALO, :] = jnp.where(j > 0, up_ref[...], 0.0)
    pad_ref[CONV_HALO:CONV_HALO + tc, :] = uc_ref[...]
    pad_ref[CONV_HALO + tc:2 * CONV_HALO + tc, :] = jnp.where(j < nj - 1, un_ref[...], 0.0)
    first = CONV_HALO - CONV_K // 2
    lane_blk = 128
    for s in range(tc // CONV_SUB):
        cols = []
        for cb in range(CONV_CH // lane_blk):
            cs = slice(cb * lane_blk, (cb + 1) * lane_blk)
            acc = None
            for k in range(CONV_K):
                r0 = s * CONV_SUB + first + k
                term = pad_ref[r0:r0 + CONV_SUB, cs] * w_ref[k:k + 1, cs]
                acc = term if acc is None else acc + term
            cols.append(acc)
        y = jnp.concatenate(cols, axis=-1) + b_ref[...]
        mu = jnp.mean(y, axis=-1, keepdims=True)
        yc = y - mu
        var = jnp.mean(yc * yc, axis=-1, keepdims=True)
        z = yc * lax.rsqrt(var + EPS) * lg_ref[...] + lb_ref[...]
        o_ref[s * CONV_SUB:(s + 1) * CONV_SUB, :] = (z * jax.nn.sigmoid(z)).astype(BF16)


def _conv_ln_swish(u, conv_w, conv_b, ln_g, ln_b):
    b, t, ch = u.shape
    tc = min(CONV_TILE, t)
    halos_per_tile = tc // CONV_HALO
    n_halo = t // CONV_HALO
    return pl.pallas_call(
        _conv_kernel,
        grid=(b, t // tc),
        in_specs=[
            pl.BlockSpec((None, CONV_HALO, ch), lambda i, j: (i, jnp.maximum(j * halos_per_tile - 1, 0), 0)),
            pl.BlockSpec((None, tc, ch), lambda i, j: (i, j, 0)),
            pl.BlockSpec((None, CONV_HALO, ch), lambda i, j: (i, jnp.minimum((j + 1) * halos_per_tile, n_halo - 1), 0)),
            _const_spec(conv_w.shape),
            _const_spec((1, ch)), _const_spec((1, ch)), _const_spec((1, ch)),
        ],
        out_specs=pl.BlockSpec((None, tc, ch), lambda i, j: (i, j, 0)),
        out_shape=jax.ShapeDtypeStruct((b, t, ch), BF16),
        scratch_shapes=[pltpu.VMEM((tc + 2 * CONV_HALO, ch), F32)],
        compiler_params=_params(2),
        name="conv_ln_swish",
    )(u, u, u, conv_w, conv_b.reshape(1, ch), ln_g.reshape(1, ch), ln_b.reshape(1, ch))


def _rope_tables(t, with_pos):
    ones_m = jnp.ones((t, MLA_QK_PAD), F32)
    if not with_pos:
        return ones_m, jnp.zeros((t, MLA_QK_PAD), F32), jnp.ones((t, GQA_HEAD_DIM), F32), jnp.zeros((t, GQA_HEAD_DIM), F32)
    tok = jnp.arange(t, dtype=jnp.int32)
    row, col = tok // GRID_W, tok % GRID_W

    def tables(half):
        freqs = ROPE_THETA ** (-jnp.arange(half, dtype=F32) / half)
        cos_parts, sin_parts = [], []
        for pos in (row, col):
            ang = pos.astype(F32)[:, None] * freqs[None, :]
            cos_parts += [jnp.cos(ang), jnp.cos(ang)]
            sin_parts += [-jnp.sin(ang), jnp.sin(ang)]
        return jnp.concatenate(cos_parts, axis=-1), jnp.concatenate(sin_parts, axis=-1)

    cos_r, sin_r = tables(MLA_ROPE // 4)
    cos_g, sin_g = tables(GQA_HEAD_DIM // 4)
    pad = MLA_QK_PAD - MLA_NOPE - MLA_ROPE
    cos_m = jnp.concatenate([jnp.ones((t, MLA_NOPE), F32), cos_r, jnp.ones((t, pad), F32)], axis=-1)
    sin_m = jnp.concatenate([jnp.zeros((t, MLA_NOPE), F32), sin_r, jnp.zeros((t, pad), F32)], axis=-1)
    return cos_m, sin_m, cos_g, sin_g


def _dft_tables(t):
    def cos_sin(n):
        idx = jnp.arange(n, dtype=jnp.int32)
        ang = ((idx[:, None] * idx[None, :]) % n).astype(F32) * (2.0 * np.pi / n)
        return jnp.cos(ang), jnp.sin(ang)

    cc, sc = cos_sin(FNET_GROUP_W)
    ct, st = cos_sin(t)
    scale = (t * FNET_GROUP_W) ** -0.5
    return (jnp.concatenate([cc, sc], axis=-1).astype(BF16),
            (jnp.concatenate([ct, -st], axis=-1) * scale).astype(BF16))


def _attn_weights(w_in, q_norm_g, w_uq, kv_norm_g, w_ukv, gq, gk, w_out):
    o1 = ATTN_Q_W + MLA_KV_RANK
    o2 = o1 + MLA_ROPE
    o3 = o2 + GQA_KV_HEADS * GQA_HEAD_DIM
    d = w_in.shape[0]
    w_in_p = jnp.concatenate(
        [w_in[:, :o1], w_in[:, o2:], w_in[:, o1:o2], jnp.zeros((d, ATTN_IN_PAD_W - w_in.shape[1]), w_in.dtype)], axis=-1)
    assert o3 - o2 == GQA_KV_HEADS * GQA_HEAD_DIM and w_in_p.shape[1] == ATTN_IN_PAD_W
    w_uq_p = jnp.pad(w_uq.reshape(MLA_Q_RANK, MLA_HEADS, MLA_NOPE + MLA_ROPE),
                     ((0, 0), (0, 0), (0, MLA_QK_PAD - MLA_NOPE - MLA_ROPE))).reshape(MLA_Q_RANK, MLA_HEADS * MLA_QK_PAD)
    ukv = w_ukv.reshape(MLA_KV_RANK, MLA_HEADS, MLA_NOPE + MLA_V)
    k_rows = jnp.pad(ukv[:, :, :MLA_NOPE], ((0, 0), (0, 0), (0, MLA_QK_PAD - MLA_NOPE)))
    route = jnp.zeros((KR_W, MLA_HEADS, MLA_QK_PAD), w_ukv.dtype)
    eye = jnp.eye(MLA_ROPE, dtype=w_ukv.dtype)
    for hd in range(MLA_HEADS):
        route = route.at[:MLA_ROPE, hd, MLA_NOPE:MLA_NOPE + MLA_ROPE].set(eye)
    k_part = jnp.concatenate([k_rows, route], axis=0).reshape(MLA_KV_RANK + KR_W, MLA_HEADS * MLA_QK_PAD)
    v_part = jnp.concatenate([ukv[:, :, MLA_NOPE:].reshape(MLA_KV_RANK, MLA_HEADS * MLA_V),
                              jnp.zeros((KR_W, MLA_HEADS * MLA_V), w_ukv.dtype)], axis=0)
    return {
        "w_in": w_in_p.astype(BF16),
        "q_norm_g": q_norm_g.reshape(1, -1),
        "w_uq": w_uq_p.astype(BF16),
        "kv_norm_g": kv_norm_g.reshape(1, -1),
        "w_ukv": jnp.concatenate([k_part, v_part], axis=-1).astype(BF16),
        "gq": gq.reshape(1, -1),
        "gk": gk.reshape(1, -1),
        "w_out": w_out.astype(BF16),
    }


def kernel(x, c, ctx, c_ctx, mod_w, mod_b, norm1_g, norm2_g, attn_w_in, mla_q_norm_g, mla_w_uq, mla_kv_norm_g,
           mla_w_ukv, gqa_q_norm_g, gqa_k_norm_g, attn_w_out, fc_w_in, conv_w, conv_b, conv_ln_g, conv_ln_b,
           fc_w_out, mlp_w1, mlp_w2, final_g):
    b, s, d = x.shape
    t_ctx = ctx.shape[1]
    depth = mod_w.shape[0]
    last_attn = ((depth - 1) // 2) * 2

    rows = -(-(b + 1) // 8) * 8
    cond = jnp.concatenate([c, c_ctx[None, :], jnp.zeros((rows - b - 1, d), F32)], axis=0)
    mod = _adaln(cond, mod_w, mod_b).reshape(depth, rows, N_MOD, d)

    tables_x = _rope_tables(s, True)
    tables_c = _rope_tables(t_ctx, False)
    final_g2 = final_g.reshape(1, d)

    for i in range(depth):
        ctx_mode = "full" if i < last_attn else ("kv" if i == last_attn else "none")
        j = i // 2
        mod_x, mod_c = mod[i, :b], mod[i, b:b + 1]
        n1g, n2g = norm1_g[i].reshape(1, d), norm2_g[i].reshape(1, d)
        w1, w2 = mlp_w1[i].astype(BF16), mlp_w2[i].astype(BF16)
        final = i == depth - 1

        if i % 2 == 0:
            w = _attn_weights(attn_w_in[j], mla_q_norm_g[j], mla_w_uq[j], mla_kv_norm_g[j], mla_w_ukv[j],
                              gqa_q_norm_g[j], gqa_k_norm_g[j], attn_w_out[j])
            qm, qg, ktm, ktg, vm, vg = _attn_in(x, mod_x, True, n1g, w, tables_x)
            segs_m, segs_g = [(ktm, vm)], [(ktg, vg)]
            if ctx_mode != "none":
                qm_c, qg_c, ktm_c, ktg_c, vm_c, vg_c = _attn_in(ctx, mod_c, False, n1g, w, tables_c)
                segs_m.insert(0, (ktm_c, vm_c))
                segs_g.insert(0, (ktg_c, vg_c))
            o_m = _attention(qm, segs_m, MLA_HEADS, 1, MLA_QK_PAD, MLA_V)
            o_g = _attention(qg, segs_g, GQA_HEADS, GQA_GROUP, GQA_HEAD_DIM, GQA_HEAD_DIM)
            x = _post(x, (o_m, 0), (o_g, 0), mod_x, True, w["w_out"], n2g, w1, w2, final_g2, final)
            if ctx_mode == "full":
                oc_m = _attention(qm_c, segs_m[:1], MLA_HEADS, 1, MLA_QK_PAD, MLA_V)
                oc_g = _attention(qg_c, segs_g[:1], GQA_HEADS, GQA_GROUP, GQA_HEAD_DIM, GQA_HEAD_DIM)
                ctx = _post(ctx, (oc_m, 0), (oc_g, 0), mod_c, False, w["w_out"], n2g, w1, w2, final_g2, False)
        else:
            w_in = fc_w_in[j].astype(BF16)
            w_out = fc_w_out[j].astype(BF16)
            streams = [(x, mod_x, True)] + ([(ctx, mod_c, False)] if ctx_mode == "full" else [])
            outs = []
            for tokens, mod_t, per_batch in streams:
                cs, w_dft = _dft_tables(tokens.shape[1])
                fcs, u = _fc_in(tokens, mod_t, per_batch, n1g, w_in, cs)
                f = _seq_dft(fcs, w_dft)
                uc = _conv_ln_swish(u, conv_w[j], conv_b[j], conv_ln_g[j], conv_ln_b[j])
                is_x = tokens is x
                outs.append(_post(tokens, (f, 0), (uc, 0), mod_t, per_batch, w_out, n2g, w1, w2, final_g2,
                                  final and is_x))
            x = outs[0]
            if ctx_mode == "full":
                ctx = outs[1]
    return x
```

```python
import functools

import jax
import jax.numpy as jnp
import numpy as np
from jax import lax
from jax.experimental import pallas as pl
from jax.experimental.pallas import tpu as pltpu

F32 = jnp.float32
BF16 = jnp.bfloat16

D_MODEL = 1024
GRID_W = 64
EPS = 1e-6
ROPE_THETA = 10000.0
N_MOD = 6

MLA_HEADS = 4
MLA_Q_RANK = D_MODEL // 2
MLA_KV_RANK = D_MODEL // 4
MLA_NOPE = 128
MLA_ROPE = 64
MLA_V = 128
MLA_QK_PAD = 256
MLA_SCALE = (MLA_NOPE + MLA_ROPE) ** -0.5

GQA_HEADS = 4
GQA_KV_HEADS = 2
GQA_GROUP = GQA_HEADS // GQA_KV_HEADS
GQA_HEAD_DIM = 128
GQA_SCALE = GQA_HEAD_DIM ** -0.5

FNET_GROUPS = 4
FNET_GROUP_W = 128
FNET_W = FNET_GROUPS * FNET_GROUP_W
CONV_CH = 512
CONV_K = 31
CONV_HALO = 16
D_FF = 4 * D_MODEL

ATTN_Q_W = MLA_Q_RANK + GQA_HEADS * GQA_HEAD_DIM
ATTN_IN_PAD_W = 1920
COL_QG = MLA_Q_RANK
COL_CKV = COL_QG + GQA_HEADS * GQA_HEAD_DIM
COL_KG = COL_CKV + MLA_KV_RANK
COL_VG = COL_KG + GQA_KV_HEADS * GQA_HEAD_DIM
COL_KR = COL_VG + GQA_KV_HEADS * GQA_HEAD_DIM
KR_W = 128

V7X_VMEM_LIMIT = 56 * 1024 * 1024
TOKEN_TILE = 512
ATTN_Q_TILE = 512
ATTN_ROW_SPLIT = 2
DFT_ROW_TILE = 512
CONV_TILE = 256
LANES = 128
SUBLANES = 8
CONV_STRIDE = 4
CONV_LN_ROWS = 64
FF_CHUNK = 1024
ADALN_COL_TILE = 1536


def _params(n_axes):
    return pltpu.CompilerParams(dimension_semantics=("arbitrary",) * n_axes, vmem_limit_bytes=V7X_VMEM_LIMIT)


def _const_spec(shape):
    zeros = (0,) * len(shape)
    return pl.BlockSpec(shape, lambda *_: zeros, pipeline_mode=pl.Buffered(1))


def _rms(x, g):
    return x * lax.rsqrt(jnp.mean(x * x, axis=-1, keepdims=True) + EPS) * g


def _norm_mod(x, g, shift, scale):
    return _rms(x, g) * (1.0 + scale) + shift


def _swap_halves(x, blk):
    w = x.shape[-1]
    ax = x.ndim - 1
    lane = lax.broadcasted_iota(jnp.int32, x.shape, ax)
    first = (lane & (2 * blk - 1)) < blk
    return jnp.where(first, pltpu.roll(x, w - blk, ax), pltpu.roll(x, blk, ax))


def _rope(x, cos, sin, blk):
    return x * cos + _swap_halves(x, blk) * sin


def _adaln_kernel(c_ref, w_ref, b_ref, o_ref):
    c = c_ref[...]
    s = (c * jax.nn.sigmoid(c)).astype(BF16)
    o_ref[...] = jnp.dot(s, w_ref[...].astype(BF16), preferred_element_type=F32) + b_ref[...]


def _adaln(cond, mod_w, mod_b):
    depth, d, n = mod_w.shape
    rows = cond.shape[0]
    tn = ADALN_COL_TILE if n % ADALN_COL_TILE == 0 else n
    return pl.pallas_call(
        _adaln_kernel,
        grid=(depth, n // tn),
        in_specs=[
            pl.BlockSpec((rows, d), lambda l, j: (0, 0)),
            pl.BlockSpec((None, d, tn), lambda l, j: (l, 0, j)),
            pl.BlockSpec((None, 1, tn), lambda l, j: (l, 0, j)),
        ],
        out_specs=pl.BlockSpec((None, rows, tn), lambda l, j: (l, 0, j)),
        out_shape=jax.ShapeDtypeStruct((depth, rows, n), F32),
        compiler_params=_params(2),
        name="adaln",
    )(cond, mod_w, mod_b.reshape(depth, 1, n))


def _attn_in_kernel(x_ref, mod_ref, n1g_ref, win_ref, qng_ref, wuq_ref, kvng_ref, wukv_ref, gq_ref, gk_ref,
                    cm_ref, sm_ref, cg_ref, sg_ref,
                    qm_ref, qg_ref, ktm_ref, ktg_ref, vm_ref, vg_ref):
    h = _norm_mod(x_ref[...], n1g_ref[...], mod_ref[0:1, :], mod_ref[1:2, :]).astype(BF16)
    p = jnp.dot(h, win_ref[...], preferred_element_type=F32)
    cm, sm = cm_ref[...], sm_ref[...]
    cg, sg = cg_ref[...], sg_ref[...]

    cq = _rms(p[:, 0:MLA_Q_RANK], qng_ref[...]).astype(BF16)
    qm = jnp.dot(cq, wuq_ref[...], preferred_element_type=F32)
    for hd in range(MLA_HEADS):
        sl = slice(hd * MLA_QK_PAD, (hd + 1) * MLA_QK_PAD)
        qm_ref[:, sl] = (_rope(qm[:, sl], cm, sm, MLA_ROPE // 4) * MLA_SCALE).astype(BF16)

    for hd in range(GQA_HEADS):
        sl = slice(hd * GQA_HEAD_DIM, (hd + 1) * GQA_HEAD_DIM)
        y = _rms(p[:, COL_QG + hd * GQA_HEAD_DIM:COL_QG + (hd + 1) * GQA_HEAD_DIM], gq_ref[...])
        qg_ref[:, sl] = (_rope(y, cg, sg, GQA_HEAD_DIM // 4) * GQA_SCALE).astype(BF16)

    ckv = _rms(p[:, COL_CKV:COL_CKV + MLA_KV_RANK], kvng_ref[...])
    kr = _rope(p[:, COL_KR:COL_KR + KR_W], cm[:, MLA_NOPE:MLA_NOPE + KR_W], sm[:, MLA_NOPE:MLA_NOPE + KR_W],
               MLA_ROPE // 4)
    lhs = jnp.concatenate([ckv, kr], axis=-1).astype(BF16)
    kv = jnp.dot(lhs, wukv_ref[...], preferred_element_type=F32)
    kw = MLA_HEADS * MLA_QK_PAD
    ktm_ref[...] = kv[:, :kw].T.astype(BF16)
    vm_ref[...] = kv[:, kw:].astype(BF16)

    for hd in range(GQA_KV_HEADS):
        y = _rms(p[:, COL_KG + hd * GQA_HEAD_DIM:COL_KG + (hd + 1) * GQA_HEAD_DIM], gk_ref[...])
        ktg_ref[hd * GQA_HEAD_DIM:(hd + 1) * GQA_HEAD_DIM, :] = _rope(y, cg, sg, GQA_HEAD_DIM // 4).T.astype(BF16)
    vg_ref[...] = p[:, COL_VG:COL_VG + GQA_KV_HEADS * GQA_HEAD_DIM].astype(BF16)


def _attn_in(x, mod, per_batch_mod, n1g, w, tables):
    b, t, d = x.shape
    tm = min(TOKEN_TILE, t)
    mod_idx = (lambda i, j: (i, 0, 0)) if per_batch_mod else (lambda i, j: (0, 0, 0))
    tok = lambda width: pl.BlockSpec((None, tm, width), lambda i, j: (i, j, 0))
    tab = lambda width: pl.BlockSpec((tm, width), lambda i, j: (j, 0))
    qk_w = MLA_HEADS * MLA_QK_PAD
    out_shapes = (
        jax.ShapeDtypeStruct((b, t, qk_w), BF16),
        jax.ShapeDtypeStruct((b, t, GQA_HEADS * GQA_HEAD_DIM), BF16),
        jax.ShapeDtypeStruct((b, qk_w, t), BF16),
        jax.ShapeDtypeStruct((b, GQA_KV_HEADS * GQA_HEAD_DIM, t), BF16),
        jax.ShapeDtypeStruct((b, t, MLA_HEADS * MLA_V), BF16),
        jax.ShapeDtypeStruct((b, t, GQA_KV_HEADS * GQA_HEAD_DIM), BF16),
    )
    out_specs = (
        tok(qk_w), tok(GQA_HEADS * GQA_HEAD_DIM),
        pl.BlockSpec((None, qk_w, tm), lambda i, j: (i, 0, j)),
        pl.BlockSpec((None, GQA_KV_HEADS * GQA_HEAD_DIM, tm), lambda i, j: (i, 0, j)),
        tok(MLA_HEADS * MLA_V), tok(GQA_KV_HEADS * GQA_HEAD_DIM),
    )
    return pl.pallas_call(
        _attn_in_kernel,
        grid=(b, t // tm),
        in_specs=[
            tok(d),
            pl.BlockSpec((None, N_MOD, d), mod_idx),
            _const_spec((1, d)),
            _const_spec(w["w_in"].shape),
            _const_spec((1, MLA_Q_RANK)),
            _const_spec(w["w_uq"].shape),
            _const_spec((1, MLA_KV_RANK)),
            _const_spec(w["w_ukv"].shape),
            _const_spec((1, GQA_HEAD_DIM)),
            _const_spec((1, GQA_HEAD_DIM)),
            tab(MLA_QK_PAD), tab(MLA_QK_PAD), tab(GQA_HEAD_DIM), tab(GQA_HEAD_DIM),
        ],
        out_specs=out_specs,
        out_shape=out_shapes,
        compiler_params=_params(2),
        name="attn_in",
    )(x, mod, n1g, w["w_in"], w["q_norm_g"], w["w_uq"], w["kv_norm_g"], w["w_ukv"], w["gq"], w["gk"], *tables)


def _lane_fold(x, op):
    out = x[:, 0:128]
    for lb in range(1, x.shape[-1] // 128):
        out = op(out, x[:, lb * 128:(lb + 1) * 128])
    return out


def _attn_kernel(*refs, nseg, heads, kv_group, dk, dv):
    q_ref, o_ref = refs[0], refs[-1]
    rows = q_ref.shape[0] // ATTN_ROW_SPLIT
    for u in range(heads):
        ku = u // kv_group
        q = q_ref[:, u * dk:(u + 1) * dk]
        scores = [jnp.dot(q, refs[1 + 2 * i][ku * dk:(ku + 1) * dk, :], preferred_element_type=F32)
                  for i in range(nseg)]
        mx = functools.reduce(jnp.maximum, [_lane_fold(s, jnp.maximum) for s in scores])
        m = jnp.max(mx, axis=-1, keepdims=True)
        probs = [jnp.exp(s - m) for s in scores]
        lsum = functools.reduce(jnp.add, [_lane_fold(p, jnp.add) for p in probs])
        denom = jnp.sum(lsum, axis=-1, keepdims=True)
        probs = [p.astype(BF16) for p in probs]
        for r in range(ATTN_ROW_SPLIT):
            rs = slice(r * rows, (r + 1) * rows)
            acc = None
            for i in range(nseg):
                part = jnp.dot(probs[i][rs, :], refs[2 + 2 * i][:, ku * dv:(ku + 1) * dv],
                               preferred_element_type=F32)
                acc = part if acc is None else acc + part
            o_ref[rs, u * dv:(u + 1) * dv] = (acc / denom[rs]).astype(BF16)


def _attention(q, segs, heads, kv_group, dk, dv):
    b, t, _ = q.shape
    tq = min(ATTN_Q_TILE, t)
    in_specs = [pl.BlockSpec((None, tq, heads * dk), lambda i, j: (i, j, 0))]
    args = [q]
    for kt, v in segs:
        in_specs.append(pl.BlockSpec((None,) + kt.shape[1:], lambda i, j: (i, 0, 0)))
        in_specs.append(pl.BlockSpec((None,) + v.shape[1:], lambda i, j: (i, 0, 0)))
        args += [kt, v]
    return pl.pallas_call(
        functools.partial(_attn_kernel, nseg=len(segs), heads=heads, kv_group=kv_group, dk=dk, dv=dv),
        grid=(b, t // tq),
        in_specs=in_specs,
        out_specs=pl.BlockSpec((None, tq, heads * dv), lambda i, j: (i, j, 0)),
        out_shape=jax.ShapeDtypeStruct((b, t, heads * dv), BF16),
        compiler_params=_params(2),
        name="attention",
    )(*args)


def _post_kernel(x_ref, ma_ref, mb_ref, mod_ref, wo_ref, n2g_ref, w1_ref, w2_ref, fg_ref, o_ref, *, final):
    half = ma_ref.shape[-1]
    mix = (jnp.dot(ma_ref[...], wo_ref[0:half, :], preferred_element_type=F32)
           + jnp.dot(mb_ref[...], wo_ref[half:2 * half, :], preferred_element_type=F32))
    x1 = x_ref[...] + mod_ref[2:3, :] * mix
    hn = _norm_mod(x1, n2g_ref[...], mod_ref[3:4, :], mod_ref[4:5, :]).astype(BF16)
    acc = None
    for c in range(w1_ref.shape[1] // FF_CHUNK):
        sl = slice(c * FF_CHUNK, (c + 1) * FF_CHUNK)
        hc = jnp.maximum(jnp.dot(hn, w1_ref[:, sl], preferred_element_type=F32), 0.0)
        part = jnp.dot((hc * hc).astype(BF16), w2_ref[sl, :], preferred_element_type=F32)
        acc = part if acc is None else acc + part
    x2 = x1 + mod_ref[5:6, :] * acc
    if final:
        x2 = _rms(x2, fg_ref[...])
    o_ref[...] = x2


def _post(x, mix_a, mix_b, mod, per_batch_mod, w_out, n2g, w1, w2, final_g, final):
    b, t, d = x.shape
    tm = min(TOKEN_TILE, t)
    half = w_out.shape[0] // 2
    mod_idx = (lambda i, j: (i, 0, 0)) if per_batch_mod else (lambda i, j: (0, 0, 0))
    (arr_a, blk_a), (arr_b, blk_b) = mix_a, mix_b
    return pl.pallas_call(
        functools.partial(_post_kernel, final=final),
        grid=(b, t // tm),
        in_specs=[
            pl.BlockSpec((None, tm, d), lambda i, j: (i, j, 0)),
            pl.BlockSpec((None, tm, half), lambda i, j: (i, j, blk_a)),
            pl.BlockSpec((None, tm, half), lambda i, j: (i, j, blk_b)),
            pl.BlockSpec((None, N_MOD, d), mod_idx),
            _const_spec(w_out.shape),
            _const_spec((1, d)),
            _const_spec(w1.shape),
            _const_spec(w2.shape),
            _const_spec((1, d)),
        ],
        out_specs=pl.BlockSpec((None, tm, d), lambda i, j: (i, j, 0)),
        out_shape=jax.ShapeDtypeStruct((b, t, d), F32),
        compiler_params=_params(2),
        name="post_mlp",
    )(x, arr_a, arr_b, mod, w_out, n2g, w1, w2, final_g)


def _fc_in_kernel(x_ref, mod_ref, n1g_ref, win_ref, cs_ref, fcs_ref, u_ref):
    h = _norm_mod(x_ref[...], n1g_ref[...], mod_ref[0:1, :], mod_ref[1:2, :]).astype(BF16)
    p = jnp.dot(h, win_ref[...], preferred_element_type=F32)
    for g in range(FNET_GROUPS):
        sl = slice(g * FNET_GROUP_W, (g + 1) * FNET_GROUP_W)
        r = jnp.dot(p[:, sl].astype(BF16), cs_ref[...], preferred_element_type=F32)
        fcs_ref[0, :, sl] = r[:, :FNET_GROUP_W].astype(BF16)
        fcs_ref[1, :, sl] = r[:, FNET_GROUP_W:].astype(BF16)
    u_ref[...] = p[:, FNET_W:FNET_W + CONV_CH] * jax.nn.sigmoid(p[:, FNET_W + CONV_CH:])


def _fc_in(x, mod, per_batch_mod, n1g, w_in, cs):
    b, t, d = x.shape
    tm = min(TOKEN_TILE, t)
    mod_idx = (lambda i, j: (i, 0, 0)) if per_batch_mod else (lambda i, j: (0, 0, 0))
    return pl.pallas_call(
        _fc_in_kernel,
        grid=(b, t // tm),
        in_specs=[
            pl.BlockSpec((None, tm, d), lambda i, j: (i, j, 0)),
            pl.BlockSpec((None, N_MOD, d), mod_idx),
            _const_spec((1, d)),
            _const_spec(w_in.shape),
            _const_spec(cs.shape),
        ],
        out_specs=(
            pl.BlockSpec((None, 2, tm, FNET_W), lambda i, j: (i, 0, j, 0)),
            pl.BlockSpec((None, tm, CONV_CH), lambda i, j: (i, j, 0)),
        ),
        out_shape=(
            jax.ShapeDtypeStruct((b, 2, t, FNET_W), BF16),
            jax.ShapeDtypeStruct((b, t, CONV_CH), F32),
        ),
        compiler_params=_params(2),
        name="fc_in",
    )(x, mod, n1g, w_in, cs)


def _dft_kernel(w_ref, f_ref, o_ref):
    o_ref[...] = jnp.dot(w_ref[...], f_ref[...], preferred_element_type=F32).astype(BF16)


def _seq_dft(fcs, w_dft):
    b, _, t, w = fcs.shape
    tk = min(DFT_ROW_TILE, t)
    return pl.pallas_call(
        _dft_kernel,
        grid=(t // tk, b),
        in_specs=[
            pl.BlockSpec((tk, 2 * t), lambda k, i: (k, 0)),
            pl.BlockSpec((None, 2 * t, w), lambda k, i: (i, 0, 0)),
        ],
        out_specs=pl.BlockSpec((None, tk, w), lambda k, i: (i, k, 0)),
        out_shape=jax.ShapeDtypeStruct((b, t, w), BF16),
        compiler_params=_params(2),
        name="seq_dft",
    )(w_dft, fcs.reshape(b, 2 * t, w))


def _conv_kernel(up_ref, uc_ref, un_ref, w_ref, b_ref, lg_ref, lb_ref, o_ref, pad_ref, z_ref):
    j, nj = pl.program_id(1), pl.num_programs(1)
    tc = uc_ref.shape[0]
    nslab = CONV_CH // LANES
    slabs = [slice(sb * LANES, (sb + 1) * LANES) for sb in range(nslab)]
    for sb, cs in enumerate(slabs):
        pad_ref[sb, 0:CONV_HALO, :] = jnp.where(j > 0, up_ref[:, cs], 0.0)
        pad_ref[sb, CONV_HALO:CONV_HALO + tc, :] = uc_ref[:, cs]
        pad_ref[sb, CONV_HALO + tc:2 * CONV_HALO + tc, :] = jnp.where(j < nj - 1, un_ref[:, cs], 0.0)
    first = CONV_HALO - CONV_K // 2
    for sb, cs in enumerate(slabs):
        taps = [jnp.broadcast_to(w_ref[k:k + 1, cs], (SUBLANES, LANES)) for k in range(CONV_K)]
        bias = jnp.broadcast_to(b_ref[:, cs], (SUBLANES, LANES))
        for r0 in range(0, tc, SUBLANES * CONV_STRIDE):
            for ph in range(CONV_STRIDE):
                acc = bias
                for k in range(CONV_K):
                    rows = pl.ds(r0 + ph + first + k, SUBLANES, stride=CONV_STRIDE)
                    acc = acc + pad_ref[sb, rows, :] * taps[k]
                z_ref[sb, pl.ds(r0 + ph, SUBLANES, stride=CONV_STRIDE), :] = acc
    inv_ch = 1.0 / CONV_CH
    for r0 in range(0, tc, CONV_LN_ROWS):
        rows = slice(r0, r0 + CONV_LN_ROWS)
        ys = [z_ref[sb, rows, :] for sb in range(nslab)]
        mu = jnp.sum(functools.reduce(jnp.add, ys), axis=-1, keepdims=True) * inv_ch
        ys = [y - mu for y in ys]
        var = jnp.sum(functools.reduce(jnp.add, [y * y for y in ys]), axis=-1, keepdims=True) * inv_ch
        rstd = lax.rsqrt(var + EPS)
        for sb, cs in enumerate(slabs):
            z = ys[sb] * rstd * lg_ref[:, cs] + lb_ref[:, cs]
            o_ref[rows, cs] = (z * jax.nn.sigmoid(z)).astype(BF16)


def _conv_ln_swish(u, conv_w, conv_b, ln_g, ln_b):
    b, t, ch = u.shape
    tc = min(CONV_TILE, t)
    halos_per_tile = tc // CONV_HALO
    n_halo = t // CONV_HALO
    return pl.pallas_call(
        _conv_kernel,
        grid=(b, t // tc),
        in_specs=[
            pl.BlockSpec((None, CONV_HALO, ch), lambda i, j: (i, jnp.maximum(j * halos_per_tile - 1, 0), 0)),
            pl.BlockSpec((None, tc, ch), lambda i, j: (i, j, 0)),
            pl.BlockSpec((None, CONV_HALO, ch), lambda i, j: (i, jnp.minimum((j + 1) * halos_per_tile, n_halo - 1), 0)),
            _const_spec(conv_w.shape),
            _const_spec((1, ch)), _const_spec((1, ch)), _const_spec((1, ch)),
        ],
        out_specs=pl.BlockSpec((None, tc, ch), lambda i, j: (i, j, 0)),
        out_shape=jax.ShapeDtypeStruct((b, t, ch), BF16),
        scratch_shapes=[pltpu.VMEM((ch // LANES, tc + 2 * CONV_HALO, LANES), F32),
                        pltpu.VMEM((ch // LANES, tc, LANES), F32)],
        compiler_params=_params(2),
        name="conv_ln_swish",
    )(u, u, u, conv_w, conv_b.reshape(1, ch), ln_g.reshape(1, ch), ln_b.reshape(1, ch))


def _rope_tables(t, with_pos):
    ones_m = jnp.ones((t, MLA_QK_PAD), F32)
    if not with_pos:
        return ones_m, jnp.zeros((t, MLA_QK_PAD), F32), jnp.ones((t, GQA_HEAD_DIM), F32), jnp.zeros((t, GQA_HEAD_DIM), F32)
    tok = jnp.arange(t, dtype=jnp.int32)
    row, col = tok // GRID_W, tok % GRID_W

    def tables(half):
        freqs = ROPE_THETA ** (-jnp.arange(half, dtype=F32) / half)
        cos_parts, sin_parts = [], []
        for pos in (row, col):
            ang = pos.astype(F32)[:, None] * freqs[None, :]
            cos_parts += [jnp.cos(ang), jnp.cos(ang)]
            sin_parts += [-jnp.sin(ang), jnp.sin(ang)]
        return jnp.concatenate(cos_parts, axis=-1), jnp.concatenate(sin_parts, axis=-1)

    cos_r, sin_r = tables(MLA_ROPE // 4)
    cos_g, sin_g = tables(GQA_HEAD_DIM // 4)
    pad = MLA_QK_PAD - MLA_NOPE - MLA_ROPE
    cos_m = jnp.concatenate([jnp.ones((t, MLA_NOPE), F32), cos_r, jnp.ones((t, pad), F32)], axis=-1)
    sin_m = jnp.concatenate([jnp.zeros((t, MLA_NOPE), F32), sin_r, jnp.zeros((t, pad), F32)], axis=-1)
    return cos_m, sin_m, cos_g, sin_g


def _dft_tables(t):
    def cos_sin(n):
        idx = jnp.arange(n, dtype=jnp.int32)
        ang = ((idx[:, None] * idx[None, :]) % n).astype(F32) * (2.0 * np.pi / n)
        return jnp.cos(ang), jnp.sin(ang)

    cc, sc = cos_sin(FNET_GROUP_W)
    ct, st = cos_sin(t)
    scale = (t * FNET_GROUP_W) ** -0.5
    return (jnp.concatenate([cc, sc], axis=-1).astype(BF16),
            (jnp.concatenate([ct, -st], axis=-1) * scale).astype(BF16))


def _attn_weights(w_in, q_norm_g, w_uq, kv_norm_g, w_ukv, gq, gk, w_out):
    o1 = ATTN_Q_W + MLA_KV_RANK
    o2 = o1 + MLA_ROPE
    o3 = o2 + GQA_KV_HEADS * GQA_HEAD_DIM
    d = w_in.shape[0]
    w_in_p = jnp.concatenate(
        [w_in[:, :o1], w_in[:, o2:], w_in[:, o1:o2], jnp.zeros((d, ATTN_IN_PAD_W - w_in.shape[1]), w_in.dtype)], axis=-1)
    assert o3 - o2 == GQA_KV_HEADS * GQA_HEAD_DIM and w_in_p.shape[1] == ATTN_IN_PAD_W
    w_uq_p = jnp.pad(w_uq.reshape(MLA_Q_RANK, MLA_HEADS, MLA_NOPE + MLA_ROPE),
                     ((0, 0), (0, 0), (0, MLA_QK_PAD - MLA_NOPE - MLA_ROPE))).reshape(MLA_Q_RANK, MLA_HEADS * MLA_QK_PAD)
    ukv = w_ukv.reshape(MLA_KV_RANK, MLA_HEADS, MLA_NOPE + MLA_V)
    k_rows = jnp.pad(ukv[:, :, :MLA_NOPE], ((0, 0), (0, 0), (0, MLA_QK_PAD - MLA_NOPE)))
    route = jnp.zeros((KR_W, MLA_HEADS, MLA_QK_PAD), w_ukv.dtype)
    eye = jnp.eye(MLA_ROPE, dtype=w_ukv.dtype)
    for hd in range(MLA_HEADS):
        route = route.at[:MLA_ROPE, hd, MLA_NOPE:MLA_NOPE + MLA_ROPE].set(eye)
    k_part = jnp.concatenate([k_rows, route], axis=0).reshape(MLA_KV_RANK + KR_W, MLA_HEADS * MLA_QK_PAD)
    v_part = jnp.concatenate([ukv[:, :, MLA_NOPE:].reshape(MLA_KV_RANK, MLA_HEADS * MLA_V),
                              jnp.zeros((KR_W, MLA_HEADS * MLA_V), w_ukv.dtype)], axis=0)
    return {
        "w_in": w_in_p.astype(BF16),
        "q_norm_g": q_norm_g.reshape(1, -1),
        "w_uq": w_uq_p.astype(BF16),
        "kv_norm_g": kv_norm_g.reshape(1, -1),
        "w_ukv": jnp.concatenate([k_part, v_part], axis=-1).astype(BF16),
        "gq": gq.reshape(1, -1),
        "gk": gk.reshape(1, -1),
        "w_out": w_out.astype(BF16),
    }


def kernel(x, c, ctx, c_ctx, mod_w, mod_b, norm1_g, norm2_g, attn_w_in, mla_q_norm_g, mla_w_uq, mla_kv_norm_g,
           mla_w_ukv, gqa_q_norm_g, gqa_k_norm_g, attn_w_out, fc_w_in, conv_w, conv_b, conv_ln_g, conv_ln_b,
           fc_w_out, mlp_w1, mlp_w2, final_g):
    b, s, d = x.shape
    t_ctx = ctx.shape[1]
    depth = mod_w.shape[0]
    last_attn = ((depth - 1) // 2) * 2

    rows = -(-(b + 1) // 8) * 8
    cond = jnp.concatenate([c, c_ctx[None, :], jnp.zeros((rows - b - 1, d), F32)], axis=0)
    mod = _adaln(cond, mod_w, mod_b).reshape(depth, rows, N_MOD, d)

    tables_x = _rope_tables(s, True)
    tables_c = _rope_tables(t_ctx, False)
    final_g2 = final_g.reshape(1, d)

    for i in range(depth):
        ctx_mode = "full" if i < last_attn else ("kv" if i == last_attn else "none")
        j = i // 2
        mod_x, mod_c = mod[i, :b], mod[i, b:b + 1]
        n1g, n2g = norm1_g[i].reshape(1, d), norm2_g[i].reshape(1, d)
        w1, w2 = mlp_w1[i].astype(BF16), mlp_w2[i].astype(BF16)
        final = i == depth - 1

        if i % 2 == 0:
            w = _attn_weights(attn_w_in[j], mla_q_norm_g[j], mla_w_uq[j], mla_kv_norm_g[j], mla_w_ukv[j],
                              gqa_q_norm_g[j], gqa_k_norm_g[j], attn_w_out[j])
            qm, qg, ktm, ktg, vm, vg = _attn_in(x, mod_x, True, n1g, w, tables_x)
            segs_m, segs_g = [(ktm, vm)], [(ktg, vg)]
            if ctx_mode != "none":
                qm_c, qg_c, ktm_c, ktg_c, vm_c, vg_c = _attn_in(ctx, mod_c, False, n1g, w, tables_c)
                segs_m.insert(0, (ktm_c, vm_c))
                segs_g.insert(0, (ktg_c, vg_c))
            o_m = _attention(qm, segs_m, MLA_HEADS, 1, MLA_QK_PAD, MLA_V)
            o_g = _attention(qg, segs_g, GQA_HEADS, GQA_GROUP, GQA_HEAD_DIM, GQA_HEAD_DIM)
            x = _post(x, (o_m, 0), (o_g, 0), mod_x, True, w["w_out"], n2g, w1, w2, final_g2, final)
            if ctx_mode == "full":
                oc_m = _attention(qm_c, segs_m[:1], MLA_HEADS, 1, MLA_QK_PAD, MLA_V)
                oc_g = _attention(qg_c, segs_g[:1], GQA_HEADS, GQA_GROUP, GQA_HEAD_DIM, GQA_HEAD_DIM)
                ctx = _post(ctx, (oc_m, 0), (oc_g, 0), mod_c, False, w["w_out"], n2g, w1, w2, final_g2, False)
        else:
            w_in = fc_w_in[j].astype(BF16)
            w_out = fc_w_out[j].astype(BF16)
            streams = [(x, mod_x, True)] + ([(ctx, mod_c, False)] if ctx_mode == "full" else [])
            outs = []
            for tokens, mod_t, per_batch in streams:
                cs, w_dft = _dft_tables(tokens.shape[1])
                fcs, u = _fc_in(tokens, mod_t, per_batch, n1g, w_in, cs)
                f = _seq_dft(fcs, w_dft)
                uc = _conv_ln_swish(u, conv_w[j], conv_b[j], conv_ln_g[j], conv_ln_b[j])
                is_x = tokens is x
                outs.append(_post(tokens, (f, 0), (uc, 0), mod_t, per_batch, w_out, n2g, w1, w2, final_g2,
                                  final and is_x))
            x = outs[0]
            if ctx_mode == "full":
                ctx = outs[1]
    return x
```

```python
import functools

import jax
import jax.numpy as jnp
import numpy as np
from jax import lax
from jax.experimental import pallas as pl
from jax.experimental.pallas import tpu as pltpu

F32 = jnp.float32
BF16 = jnp.bfloat16

D_MODEL = 1024
GRID_W = 64
EPS = 1e-6
ROPE_THETA = 10000.0
N_MOD = 6

MLA_HEADS = 4
MLA_Q_RANK = D_MODEL // 2
MLA_KV_RANK = D_MODEL // 4
MLA_NOPE = 128
MLA_ROPE = 64
MLA_V = 128
V_PAD = 256
MLA_QK_PAD = 256
LOG2_E = float(np.log2(np.e))
MLA_SCALE = (MLA_NOPE + MLA_ROPE) ** -0.5 * LOG2_E

GQA_HEADS = 4
GQA_KV_HEADS = 2
GQA_GROUP = GQA_HEADS // GQA_KV_HEADS
GQA_HEAD_DIM = 128
GQA_SCALE = GQA_HEAD_DIM ** -0.5 * LOG2_E

FNET_GROUPS = 4
FNET_GROUP_W = 128
FNET_W = FNET_GROUPS * FNET_GROUP_W
CONV_CH = 512
CONV_K = 31
CONV_HALO = 16
D_FF = 4 * D_MODEL

ATTN_Q_W = MLA_Q_RANK + GQA_HEADS * GQA_HEAD_DIM
ATTN_IN_PAD_W = 1920
COL_QG = MLA_Q_RANK
COL_CKV = COL_QG + GQA_HEADS * GQA_HEAD_DIM
COL_KG = COL_CKV + MLA_KV_RANK
COL_VG = COL_KG + GQA_KV_HEADS * GQA_HEAD_DIM
COL_KR = COL_VG + GQA_KV_HEADS * GQA_HEAD_DIM
KR_W = 128

V7X_VMEM_LIMIT = 56 * 1024 * 1024
TOKEN_TILE = 512
ROW_SPLIT = 2
ATTN_Q_TILE = 1024
ATTN_Q_SUB = 512
ATTN_ROW_SPLIT = 2
DFT_ROW_TILE = 512
CONV_TILE = 256
LANES = 128
SUBLANES = 8
CONV_STRIDE = 4
CONV_LN_ROWS = 64
FF_CHUNK = 1024
ADALN_COL_TILE = 1536


def _params(n_axes):
    return pltpu.CompilerParams(dimension_semantics=("arbitrary",) * n_axes, vmem_limit_bytes=V7X_VMEM_LIMIT)


def _const_spec(shape):
    zeros = (0,) * len(shape)
    return pl.BlockSpec(shape, lambda *_: zeros, pipeline_mode=pl.Buffered(1))


def _rms(x, g):
    return x * lax.rsqrt(jnp.mean(x * x, axis=-1, keepdims=True) + EPS) * g


def _norm_mod(x, g, shift, scale):
    return _rms(x, g) * (1.0 + scale) + shift


def _swap_halves(x, blk):
    w = x.shape[-1]
    ax = x.ndim - 1
    lane = lax.broadcasted_iota(jnp.int32, x.shape, ax)
    first = (lane & (2 * blk - 1)) < blk
    return jnp.where(first, pltpu.roll(x, w - blk, ax), pltpu.roll(x, blk, ax))


def _rope(x, cos, sin, blk):
    return x * cos + _swap_halves(x, blk) * sin


def _adaln_kernel(c_ref, w_ref, b_ref, o_ref):
    c = c_ref[...]
    s = (c * jax.nn.sigmoid(c)).astype(BF16)
    o_ref[...] = jnp.dot(s, w_ref[...].astype(BF16), preferred_element_type=F32) + b_ref[...]


def _adaln(cond, mod_w, mod_b):
    depth, d, n = mod_w.shape
    rows = cond.shape[0]
    tn = ADALN_COL_TILE if n % ADALN_COL_TILE == 0 else n
    return pl.pallas_call(
        _adaln_kernel,
        grid=(depth, n // tn),
        in_specs=[
            pl.BlockSpec((rows, d), lambda l, j: (0, 0)),
            pl.BlockSpec((None, d, tn), lambda l, j: (l, 0, j)),
            pl.BlockSpec((None, 1, tn), lambda l, j: (l, 0, j)),
        ],
        out_specs=pl.BlockSpec((None, rows, tn), lambda l, j: (l, 0, j)),
        out_shape=jax.ShapeDtypeStruct((depth, rows, n), F32),
        compiler_params=_params(2),
        name="adaln",
    )(cond, mod_w, mod_b.reshape(depth, 1, n))


def _attn_in_kernel(x_ref, mod_ref, n1g_ref, win_ref, qng_ref, wuq_ref, kvng_ref, wukv_ref, gq_ref, gk_ref,
                    cr_ref, sr_ref, cg_ref, sg_ref,
                    qm_ref, qg_ref, ktm_ref, ktg_ref, vm_ref, vg_ref):
    tm = x_ref.shape[0]
    nsplit = ROW_SPLIT if tm % (ROW_SPLIT * LANES) == 0 else 1
    sub = tm // nsplit
    nope_w = MLA_HEADS * MLA_NOPE
    for r in range(nsplit):
        rs = slice(r * sub, (r + 1) * sub)
        h = _norm_mod(x_ref[rs, :], n1g_ref[...], mod_ref[0:1, :], mod_ref[1:2, :]).astype(BF16)
        p = jnp.dot(h, win_ref[...], preferred_element_type=F32)
        cr, sr = cr_ref[rs, :], sr_ref[rs, :]
        cg, sg = cg_ref[rs, :], sg_ref[rs, :]

        cq = _rms(p[:, 0:MLA_Q_RANK], qng_ref[...]).astype(BF16)
        qm = jnp.dot(cq, wuq_ref[...], preferred_element_type=F32)
        pairs = [(_rope(qm[:, nope_w + pr * LANES:nope_w + (pr + 1) * LANES], cr, sr, MLA_ROPE // 4)
                  * MLA_SCALE).astype(BF16) for pr in range(MLA_HEADS // 2)]
        for hd in range(MLA_HEADS):
            c0 = hd * MLA_QK_PAD
            qm_ref[rs, c0:c0 + MLA_NOPE] = (qm[:, hd * MLA_NOPE:(hd + 1) * MLA_NOPE] * MLA_SCALE).astype(BF16)
            qm_ref[rs, c0 + MLA_NOPE:c0 + MLA_QK_PAD] = pairs[hd // 2]

        for hd in range(GQA_HEADS):
            sl = slice(hd * GQA_HEAD_DIM, (hd + 1) * GQA_HEAD_DIM)
            y = _rms(p[:, COL_QG + hd * GQA_HEAD_DIM:COL_QG + (hd + 1) * GQA_HEAD_DIM], gq_ref[...])
            qg_ref[rs, sl] = (_rope(y, cg, sg, GQA_HEAD_DIM // 4) * GQA_SCALE).astype(BF16)

        ckv = _rms(p[:, COL_CKV:COL_CKV + MLA_KV_RANK], kvng_ref[...]).astype(BF16)
        kv = jnp.dot(ckv, wukv_ref[...], preferred_element_type=F32)
        kr_even = _rope(p[:, COL_KR:COL_KR + KR_W], cr, sr, MLA_ROPE // 4)
        kr_odd = pltpu.roll(kr_even, MLA_ROPE, 1)
        kr_t = [kr_even.T.astype(BF16), kr_odd.T.astype(BF16)]
        for hd in range(MLA_HEADS):
            c0 = hd * MLA_QK_PAD
            ktm_ref[c0:c0 + MLA_NOPE, rs] = kv[:, hd * MLA_NOPE:(hd + 1) * MLA_NOPE].T.astype(BF16)
            ktm_ref[c0 + MLA_NOPE:c0 + MLA_QK_PAD, rs] = kr_t[hd % 2]
        ones = jnp.ones((sub, V_PAD - MLA_V), BF16)
        for hd in range(MLA_HEADS):
            vm_ref[rs, hd * V_PAD:hd * V_PAD + MLA_V] = (
                kv[:, nope_w + hd * MLA_V:nope_w + (hd + 1) * MLA_V].astype(BF16))
            vm_ref[rs, hd * V_PAD + MLA_V:(hd + 1) * V_PAD] = ones

        for hd in range(GQA_KV_HEADS):
            y = _rms(p[:, COL_KG + hd * GQA_HEAD_DIM:COL_KG + (hd + 1) * GQA_HEAD_DIM], gk_ref[...])
            ktg_ref[hd * GQA_HEAD_DIM:(hd + 1) * GQA_HEAD_DIM, rs] = (
                _rope(y, cg, sg, GQA_HEAD_DIM // 4).T.astype(BF16))
        for hd in range(GQA_KV_HEADS):
            vg_ref[rs, hd * V_PAD:hd * V_PAD + GQA_HEAD_DIM] = (
                p[:, COL_VG + hd * GQA_HEAD_DIM:COL_VG + (hd + 1) * GQA_HEAD_DIM].astype(BF16))
            vg_ref[rs, hd * V_PAD + GQA_HEAD_DIM:(hd + 1) * V_PAD] = ones


def _attn_in(x, mod, per_batch_mod, n1g, w, tables):
    b, t, d = x.shape
    tm = min(TOKEN_TILE, t)
    mod_idx = (lambda i, j: (i, 0, 0)) if per_batch_mod else (lambda i, j: (0, 0, 0))
    tok = lambda width: pl.BlockSpec((None, tm, width), lambda i, j: (i, j, 0))
    tab = lambda width: pl.BlockSpec((tm, width), lambda i, j: (j, 0))
    qk_w = MLA_HEADS * MLA_QK_PAD
    out_shapes = (
        jax.ShapeDtypeStruct((b, t, qk_w), BF16),
        jax.ShapeDtypeStruct((b, t, GQA_HEADS * GQA_HEAD_DIM), BF16),
        jax.ShapeDtypeStruct((b, qk_w, t), BF16),
        jax.ShapeDtypeStruct((b, GQA_KV_HEADS * GQA_HEAD_DIM, t), BF16),
        jax.ShapeDtypeStruct((b, t, MLA_HEADS * V_PAD), BF16),
        jax.ShapeDtypeStruct((b, t, GQA_KV_HEADS * V_PAD), BF16),
    )
    out_specs = (
        tok(qk_w), tok(GQA_HEADS * GQA_HEAD_DIM),
        pl.BlockSpec((None, qk_w, tm), lambda i, j: (i, 0, j)),
        pl.BlockSpec((None, GQA_KV_HEADS * GQA_HEAD_DIM, tm), lambda i, j: (i, 0, j)),
        tok(MLA_HEADS * V_PAD), tok(GQA_KV_HEADS * V_PAD),
    )
    return pl.pallas_call(
        _attn_in_kernel,
        grid=(b, t // tm),
        in_specs=[
            tok(d),
            pl.BlockSpec((None, N_MOD, d), mod_idx),
            _const_spec((1, d)),
            _const_spec(w["w_in"].shape),
            _const_spec((1, MLA_Q_RANK)),
            _const_spec(w["w_uq"].shape),
            _const_spec((1, MLA_KV_RANK)),
            _const_spec(w["w_ukv"].shape),
            _const_spec((1, GQA_HEAD_DIM)),
            _const_spec((1, GQA_HEAD_DIM)),
            tab(LANES), tab(LANES), tab(GQA_HEAD_DIM), tab(GQA_HEAD_DIM),
        ],
        out_specs=out_specs,
        out_shape=out_shapes,
        compiler_params=_params(2),
        name="attn_in",
    )(x, mod, n1g, w["w_in"], w["q_norm_g"], w["w_uq"], w["kv_norm_g"], w["w_ukv"], w["gq"], w["gk"], *tables)


def _lane_fold(x, op):
    out = x[:, 0:128]
    for lb in range(1, x.shape[-1] // 128):
        out = op(out, x[:, lb * 128:(lb + 1) * 128])
    return out


def _attn_kernel(*refs, nseg, heads, kv_group, dk, dv):
    q_ref, o_ref = refs[0], refs[-1]
    tq = q_ref.shape[0]
    qs = min(ATTN_Q_SUB, tq)
    rows = qs // ATTN_ROW_SPLIT
    for q0 in range(0, tq, qs):
        for u in range(heads):
            ku = u // kv_group
            q = q_ref[q0:q0 + qs, u * dk:(u + 1) * dk]
            scores = [jnp.dot(q, refs[1 + 2 * i][ku * dk:(ku + 1) * dk, :], preferred_element_type=F32)
                      for i in range(nseg)]
            mx = functools.reduce(jnp.maximum, [_lane_fold(s, jnp.maximum) for s in scores])
            m = jnp.max(mx, axis=-1, keepdims=True)
            probs = [jnp.exp2(s - m).astype(BF16) for s in scores]
            for r in range(ATTN_ROW_SPLIT):
                rs = slice(r * rows, (r + 1) * rows)
                acc = None
                for i in range(nseg):
                    part = jnp.dot(probs[i][rs, :], refs[2 + 2 * i][:, ku * V_PAD:(ku + 1) * V_PAD],
                                   preferred_element_type=F32)
                    acc = part if acc is None else acc + part
                o_ref[q0 + r * rows:q0 + (r + 1) * rows, u * dv:(u + 1) * dv] = (
                    acc[:, :dv] / acc[:, dv:2 * dv]).astype(BF16)


def _attention(q, segs, heads, kv_group, dk, dv):
    b, t, _ = q.shape
    tq = min(ATTN_Q_TILE, t)
    in_specs = [pl.BlockSpec((None, tq, heads * dk), lambda i, j: (i, j, 0))]
    args = [q]
    for kt, v in segs:
        in_specs.append(pl.BlockSpec((None,) + kt.shape[1:], lambda i, j: (i, 0, 0)))
        in_specs.append(pl.BlockSpec((None,) + v.shape[1:], lambda i, j: (i, 0, 0)))
        args += [kt, v]
    return pl.pallas_call(
        functools.partial(_attn_kernel, nseg=len(segs), heads=heads, kv_group=kv_group, dk=dk, dv=dv),
        grid=(b, t // tq),
        in_specs=in_specs,
        out_specs=pl.BlockSpec((None, tq, heads * dv), lambda i, j: (i, j, 0)),
        out_shape=jax.ShapeDtypeStruct((b, t, heads * dv), BF16),
        compiler_params=_params(2),
        name="attention",
    )(*args)


def _post_kernel(x_ref, ma_ref, mb_ref, mod_ref, wo_ref, n2g_ref, w1_ref, w2_ref, fg_ref, o_ref, *, final):
    half = ma_ref.shape[-1]
    mix = (jnp.dot(ma_ref[...], wo_ref[0:half, :], preferred_element_type=F32)
           + jnp.dot(mb_ref[...], wo_ref[half:2 * half, :], preferred_element_type=F32))
    x1 = x_ref[...] + mod_ref[2:3, :] * mix
    hn = _norm_mod(x1, n2g_ref[...], mod_ref[3:4, :], mod_ref[4:5, :]).astype(BF16)
    acc = None
    for c in range(w1_ref.shape[1] // FF_CHUNK):
        sl = slice(c * FF_CHUNK, (c + 1) * FF_CHUNK)
        hc = jnp.maximum(jnp.dot(hn, w1_ref[:, sl], preferred_element_type=F32), 0.0)
        part = jnp.dot((hc * hc).astype(BF16), w2_ref[sl, :], preferred_element_type=F32)
        acc = part if acc is None else acc + part
    x2 = x1 + mod_ref[5:6, :] * acc
    if final:
        x2 = _rms(x2, fg_ref[...])
    o_ref[...] = x2


def _post(x, mix_a, mix_b, mod, per_batch_mod, w_out, n2g, w1, w2, final_g, final):
    b, t, d = x.shape
    tm = min(TOKEN_TILE, t)
    half = w_out.shape[0] // 2
    mod_idx = (lambda i, j: (i, 0, 0)) if per_batch_mod else (lambda i, j: (0, 0, 0))
    (arr_a, blk_a), (arr_b, blk_b) = mix_a, mix_b
    return pl.pallas_call(
        functools.partial(_post_kernel, final=final),
        grid=(b, t // tm),
        in_specs=[
            pl.BlockSpec((None, tm, d), lambda i, j: (i, j, 0)),
            pl.BlockSpec((None, tm, half), lambda i, j: (i, j, blk_a)),
            pl.BlockSpec((None, tm, half), lambda i, j: (i, j, blk_b)),
            pl.BlockSpec((None, N_MOD, d), mod_idx),
            _const_spec(w_out.shape),
            _const_spec((1, d)),
            _const_spec(w1.shape),
            _const_spec(w2.shape),
            _const_spec((1, d)),
        ],
        out_specs=pl.BlockSpec((None, tm, d), lambda i, j: (i, j, 0)),
        out_shape=jax.ShapeDtypeStruct((b, t, d), F32),
        compiler_params=_params(2),
        name="post_mlp",
    )(x, arr_a, arr_b, mod, w_out, n2g, w1, w2, final_g)


def _fc_in_kernel(x_ref, mod_ref, n1g_ref, win_ref, cs_ref, fcs_ref, u_ref):
    tm = x_ref.shape[0]
    nsplit = ROW_SPLIT if tm % (ROW_SPLIT * LANES) == 0 else 1
    sub = tm // nsplit
    for r in range(nsplit):
        rs = slice(r * sub, (r + 1) * sub)
        h = _norm_mod(x_ref[rs, :], n1g_ref[...], mod_ref[0:1, :], mod_ref[1:2, :]).astype(BF16)
        p = jnp.dot(h, win_ref[...], preferred_element_type=F32)
        for g in range(FNET_GROUPS):
            sl = slice(g * FNET_GROUP_W, (g + 1) * FNET_GROUP_W)
            f = jnp.dot(p[:, sl].astype(BF16), cs_ref[...], preferred_element_type=F32)
            fcs_ref[0, rs, sl] = f[:, :FNET_GROUP_W].astype(BF16)
            fcs_ref[1, rs, sl] = f[:, FNET_GROUP_W:].astype(BF16)
        u_ref[rs, :] = p[:, FNET_W:FNET_W + CONV_CH] * jax.nn.sigmoid(p[:, FNET_W + CONV_CH:])


def _fc_in(x, mod, per_batch_mod, n1g, w_in, cs):
    b, t, d = x.shape
    tm = min(TOKEN_TILE, t)
    mod_idx = (lambda i, j: (i, 0, 0)) if per_batch_mod else (lambda i, j: (0, 0, 0))
    return pl.pallas_call(
        _fc_in_kernel,
        grid=(b, t // tm),
        in_specs=[
            pl.BlockSpec((None, tm, d), lambda i, j: (i, j, 0)),
            pl.BlockSpec((None, N_MOD, d), mod_idx),
            _const_spec((1, d)),
            _const_spec(w_in.shape),
            _const_spec(cs.shape),
        ],
        out_specs=(
            pl.BlockSpec((None, 2, tm, FNET_W), lambda i, j: (i, 0, j, 0)),
            pl.BlockSpec((None, tm, CONV_CH), lambda i, j: (i, j, 0)),
        ),
        out_shape=(
            jax.ShapeDtypeStruct((b, 2, t, FNET_W), BF16),
            jax.ShapeDtypeStruct((b, t, CONV_CH), F32),
        ),
        compiler_params=_params(2),
        name="fc_in",
    )(x, mod, n1g, w_in, cs)


def _dft_kernel(w_ref, f_ref, o_ref):
    o_ref[...] = jnp.dot(w_ref[...], f_ref[...], preferred_element_type=F32).astype(BF16)


def _seq_dft(fcs, w_dft):
    b, _, t, w = fcs.shape
    tk = min(DFT_ROW_TILE, t)
    return pl.pallas_call(
        _dft_kernel,
        grid=(t // tk, b),
        in_specs=[
            pl.BlockSpec((tk, 2 * t), lambda k, i: (k, 0)),
            pl.BlockSpec((None, 2 * t, w), lambda k, i: (i, 0, 0)),
        ],
        out_specs=pl.BlockSpec((None, tk, w), lambda k, i: (i, k, 0)),
        out_shape=jax.ShapeDtypeStruct((b, t, w), BF16),
        compiler_params=_params(2),
        name="seq_dft",
    )(w_dft, fcs.reshape(b, 2 * t, w))


def _conv_kernel(up_ref, uc_ref, un_ref, w_ref, b_ref, lg_ref, lb_ref, o_ref, pad_ref, z_ref):
    j, nj = pl.program_id(1), pl.num_programs(1)
    tc = uc_ref.shape[0]
    nslab = CONV_CH // LANES
    slabs = [slice(sb * LANES, (sb + 1) * LANES) for sb in range(nslab)]
    for sb, cs in enumerate(slabs):
        pad_ref[sb, 0:CONV_HALO, :] = jnp.where(j > 0, up_ref[:, cs], 0.0)
        pad_ref[sb, CONV_HALO:CONV_HALO + tc, :] = uc_ref[:, cs]
        pad_ref[sb, CONV_HALO + tc:2 * CONV_HALO + tc, :] = jnp.where(j < nj - 1, un_ref[:, cs], 0.0)
    first = CONV_HALO - CONV_K // 2
    for sb, cs in enumerate(slabs):
        taps = [jnp.broadcast_to(w_ref[k:k + 1, cs], (SUBLANES, LANES)) for k in range(CONV_K)]
        bias = jnp.broadcast_to(b_ref[:, cs], (SUBLANES, LANES))
        for r0 in range(0, tc, SUBLANES * CONV_STRIDE):
            for ph in range(CONV_STRIDE):
                acc = bias
                for k in range(CONV_K):
                    rows = pl.ds(r0 + ph + first + k, SUBLANES, stride=CONV_STRIDE)
                    acc = acc + pad_ref[sb, rows, :] * taps[k]
                z_ref[sb, pl.ds(r0 + ph, SUBLANES, stride=CONV_STRIDE), :] = acc
    inv_ch = 1.0 / CONV_CH
    for r0 in range(0, tc, CONV_LN_ROWS):
        rows = slice(r0, r0 + CONV_LN_ROWS)
        ys = [z_ref[sb, rows, :] for sb in range(nslab)]
        mu = jnp.sum(functools.reduce(jnp.add, ys), axis=-1, keepdims=True) * inv_ch
        ys = [y - mu for y in ys]
        var = jnp.sum(functools.reduce(jnp.add, [y * y for y in ys]), axis=-1, keepdims=True) * inv_ch
        rstd = lax.rsqrt(var + EPS)
        for sb, cs in enumerate(slabs):
            z = ys[sb] * rstd * lg_ref[:, cs] + lb_ref[:, cs]
            o_ref[rows, cs] = (z * jax.nn.sigmoid(z)).astype(BF16)


def _conv_ln_swish(u, conv_w, conv_b, ln_g, ln_b):
    b, t, ch = u.shape
    tc = min(CONV_TILE, t)
    halos_per_tile = tc // CONV_HALO
    n_halo = t // CONV_HALO
    return pl.pallas_call(
        _conv_kernel,
        grid=(b, t // tc),
        in_specs=[
            pl.BlockSpec((None, CONV_HALO, ch), lambda i, j: (i, jnp.maximum(j * halos_per_tile - 1, 0), 0)),
            pl.BlockSpec((None, tc, ch), lambda i, j: (i, j, 0)),
            pl.BlockSpec((None, CONV_HALO, ch), lambda i, j: (i, jnp.minimum((j + 1) * halos_per_tile, n_halo - 1), 0)),
            _const_spec(conv_w.shape),
            _const_spec((1, ch)), _const_spec((1, ch)), _const_spec((1, ch)),
        ],
        out_specs=pl.BlockSpec((None, tc, ch), lambda i, j: (i, j, 0)),
        out_shape=jax.ShapeDtypeStruct((b, t, ch), BF16),
        scratch_shapes=[pltpu.VMEM((ch // LANES, tc + 2 * CONV_HALO, LANES), F32),
                        pltpu.VMEM((ch // LANES, tc, LANES), F32)],
        compiler_params=_params(2),
        name="conv_ln_swish",
    )(u, u, u, conv_w, conv_b.reshape(1, ch), ln_g.reshape(1, ch), ln_b.reshape(1, ch))


def _rope_tables(t, with_pos):
    if not with_pos:
        ones, zeros = jnp.ones((t, LANES), F32), jnp.zeros((t, LANES), F32)
        return ones, zeros, ones, zeros
    tok = jnp.arange(t, dtype=jnp.int32)
    row, col = tok // GRID_W, tok % GRID_W

    def tables(half):
        freqs = ROPE_THETA ** (-jnp.arange(half, dtype=F32) / half)
        cos_parts, sin_parts = [], []
        for pos in (row, col):
            ang = pos.astype(F32)[:, None] * freqs[None, :]
            cos_parts += [jnp.cos(ang), jnp.cos(ang)]
            sin_parts += [-jnp.sin(ang), jnp.sin(ang)]
        return jnp.concatenate(cos_parts, axis=-1), jnp.concatenate(sin_parts, axis=-1)

    cos_r, sin_r = tables(MLA_ROPE // 4)
    cos_g, sin_g = tables(GQA_HEAD_DIM // 4)
    return jnp.concatenate([cos_r, cos_r], axis=-1), jnp.concatenate([sin_r, sin_r], axis=-1), cos_g, sin_g


def _dft_tables(t):
    def cos_sin(n):
        idx = jnp.arange(n, dtype=jnp.int32)
        ang = ((idx[:, None] * idx[None, :]) % n).astype(F32) * (2.0 * np.pi / n)
        return jnp.cos(ang), jnp.sin(ang)

    cc, sc = cos_sin(FNET_GROUP_W)
    ct, st = cos_sin(t)
    scale = (t * FNET_GROUP_W) ** -0.5
    return (jnp.concatenate([cc, sc], axis=-1).astype(BF16),
            (jnp.concatenate([ct, -st], axis=-1) * scale).astype(BF16))


def _attn_weights(w_in, q_norm_g, w_uq, kv_norm_g, w_ukv, gq, gk, w_out):
    o1 = ATTN_Q_W + MLA_KV_RANK
    o2 = o1 + MLA_ROPE
    o3 = o2 + GQA_KV_HEADS * GQA_HEAD_DIM
    d = w_in.shape[0]
    w_in_p = jnp.concatenate(
        [w_in[:, :o1], w_in[:, o2:], w_in[:, o1:o2], jnp.zeros((d, ATTN_IN_PAD_W - w_in.shape[1]), w_in.dtype)], axis=-1)
    assert o3 - o2 == GQA_KV_HEADS * GQA_HEAD_DIM and w_in_p.shape[1] == ATTN_IN_PAD_W
    uq = w_uq.reshape(MLA_Q_RANK, MLA_HEADS, MLA_NOPE + MLA_ROPE)
    w_uq_p = jnp.concatenate([uq[:, :, :MLA_NOPE].reshape(MLA_Q_RANK, MLA_HEADS * MLA_NOPE),
                              uq[:, :, MLA_NOPE:].reshape(MLA_Q_RANK, MLA_HEADS * MLA_ROPE)], axis=-1)
    ukv = w_ukv.reshape(MLA_KV_RANK, MLA_HEADS, MLA_NOPE + MLA_V)
    w_ukv_p = jnp.concatenate([ukv[:, :, :MLA_NOPE].reshape(MLA_KV_RANK, MLA_HEADS * MLA_NOPE),
                               ukv[:, :, MLA_NOPE:].reshape(MLA_KV_RANK, MLA_HEADS * MLA_V)], axis=-1)
    return {
        "w_in": w_in_p.astype(BF16),
        "q_norm_g": q_norm_g.reshape(1, -1),
        "w_uq": w_uq_p.astype(BF16),
        "kv_norm_g": kv_norm_g.reshape(1, -1),
        "w_ukv": w_ukv_p.astype(BF16),
        "gq": gq.reshape(1, -1),
        "gk": gk.reshape(1, -1),
        "w_out": w_out.astype(BF16),
    }


def kernel(x, c, ctx, c_ctx, mod_w, mod_b, norm1_g, norm2_g, attn_w_in, mla_q_norm_g, mla_w_uq, mla_kv_norm_g,
           mla_w_ukv, gqa_q_norm_g, gqa_k_norm_g, attn_w_out, fc_w_in, conv_w, conv_b, conv_ln_g, conv_ln_b,
           fc_w_out, mlp_w1, mlp_w2, final_g):
    b, s, d = x.shape
    t_ctx = ctx.shape[1]
    depth = mod_w.shape[0]
    last_attn = ((depth - 1) // 2) * 2

    rows = -(-(b + 1) // 8) * 8
    cond = jnp.concatenate([c, c_ctx[None, :], jnp.zeros((rows - b - 1, d), F32)], axis=0)
    mod = _adaln(cond, mod_w, mod_b).reshape(depth, rows, N_MOD, d)

    tables_x = _rope_tables(s, True)
    tables_c = _rope_tables(t_ctx, False)
    final_g2 = final_g.reshape(1, d)

    for i in range(depth):
        ctx_mode = "full" if i < last_attn else ("kv" if i == last_attn else "none")
        j = i // 2
        mod_x, mod_c = mod[i, :b], mod[i, b:b + 1]
        n1g, n2g = norm1_g[i].reshape(1, d), norm2_g[i].reshape(1, d)
        w1, w2 = mlp_w1[i].astype(BF16), mlp_w2[i].astype(BF16)
        final = i == depth - 1

        if i % 2 == 0:
            w = _attn_weights(attn_w_in[j], mla_q_norm_g[j], mla_w_uq[j], mla_kv_norm_g[j], mla_w_ukv[j],
                              gqa_q_norm_g[j], gqa_k_norm_g[j], attn_w_out[j])
            qm, qg, ktm, ktg, vm, vg = _attn_in(x, mod_x, True, n1g, w, tables_x)
            segs_m, segs_g = [(ktm, vm)], [(ktg, vg)]
            if ctx_mode != "none":
                qm_c, qg_c, ktm_c, ktg_c, vm_c, vg_c = _attn_in(ctx, mod_c, False, n1g, w, tables_c)
                segs_m.insert(0, (ktm_c, vm_c))
                segs_g.insert(0, (ktg_c, vg_c))
            o_m = _attention(qm, segs_m, MLA_HEADS, 1, MLA_QK_PAD, MLA_V)
            o_g = _attention(qg, segs_g, GQA_HEADS, GQA_GROUP, GQA_HEAD_DIM, GQA_HEAD_DIM)
            x = _post(x, (o_m, 0), (o_g, 0), mod_x, True, w["w_out"], n2g, w1, w2, final_g2, final)
            if ctx_mode == "full":
                oc_m = _attention(qm_c, segs_m[:1], MLA_HEADS, 1, MLA_QK_PAD, MLA_V)
                oc_g = _attention(qg_c, segs_g[:1], GQA_HEADS, GQA_GROUP, GQA_HEAD_DIM, GQA_HEAD_DIM)
                ctx = _post(ctx, (oc_m, 0), (oc_g, 0), mod_c, False, w["w_out"], n2g, w1, w2, final_g2, False)
        else:
            w_in = fc_w_in[j].astype(BF16)
            w_out = fc_w_out[j].astype(BF16)
            streams = [(x, mod_x, True)] + ([(ctx, mod_c, False)] if ctx_mode == "full" else [])
            outs = []
            for tokens, mod_t, per_batch in streams:
                cs, w_dft = _dft_tables(tokens.shape[1])
                fcs, u = _fc_in(tokens, mod_t, per_batch, n1g, w_in, cs)
                f = _seq_dft(fcs, w_dft)
                uc = _conv_ln_swish(u, conv_w[j], conv_b[j], conv_ln_g[j], conv_ln_b[j])
                is_x = tokens is x
                outs.append(_post(tokens, (f, 0), (uc, 0), mod_t, per_batch, w_out, n2g, w1, w2, final_g2,
                                  final and is_x))
            x = outs[0]
            if ctx_mode == "full":
                ctx = outs[1]
    return x
```
